```python
import jax, jax.numpy as jnp
from jax import lax
import numpy as np

D_MODEL = 2048
BATCH = 8
SEQ = 4096
DEPTH = 4

GRID_W = 64
CTX_LEN = 256
N_HEADS = 16
QK_NOPE = 128
QK_ROPE = 64
QK_HEAD = QK_NOPE + QK_ROPE
V_HEAD = 128
KV_RANK = 512
ROPE_THETA = 10000.0
Q_BLOCK = 128
D_RNN = D_MODEL
RG_BLOCKS = 8
RG_BW = D_RNN // RG_BLOCKS
CONV_W = 4
CONV_PAD_L = 2
RG_C = 8.0
N_EXPERTS = 16
N_GROUPS = 4
EXPERTS_PER_GROUP = N_EXPERTS // N_GROUPS
TOP_K = 2
D_EXPERT = 1408
EPS = 1e-6
N_MOD = 6
IN_SPLITS = (N_HEADS * QK_HEAD, KV_RANK, QK_ROPE, D_RNN, D_RNN, D_MODEL, D_MODEL)
IN_WIDTH = sum(IN_SPLITS)

kernel_name = 'hybrid_mla_rglru_grouped_moe_dit'


def rms_norm(x, g):
    xf = x.astype(jnp.float32)
    y = xf * lax.rsqrt(jnp.mean(xf * xf, axis=-1, keepdims=True) + EPS)
    return y.astype(x.dtype) * g


def modulate(h, shift, scale):
    return h * (1.0 + scale) + shift


def rope_tables(n_tok):
    rows = n_tok // GRID_W
    row = jnp.broadcast_to(jnp.arange(rows)[:, None], (rows, GRID_W)).reshape(-1).astype(jnp.float32)
    col = jnp.broadcast_to(jnp.arange(GRID_W)[None, :], (rows, GRID_W)).reshape(-1).astype(jnp.float32)
    half = QK_ROPE // 2
    inv = ROPE_THETA ** (-jnp.arange(0, half, 2, dtype=jnp.float32) / half)
    ang_r = row[:, None] * inv
    ang_c = col[:, None] * inv
    ang = jnp.concatenate([ang_r, ang_r, ang_c, ang_c], axis=-1)
    return jnp.cos(ang), jnp.sin(ang)


def apply_rope(x, cos, sin):
    def rot(v):
        v1, v2 = jnp.split(v, 2, axis=-1)
        return jnp.concatenate([-v2, v1], axis=-1)
    xr, xc = jnp.split(x, 2, axis=-1)
    rotated = jnp.concatenate([rot(xr), rot(xc)], axis=-1)
    return (x * cos[:, None, :] + rotated * sin[:, None, :]).astype(x.dtype)


def combined_projection(h, w_in):
    offsets = [int(o) for o in np.cumsum(IN_SPLITS)[:-1]]
    return jnp.split(h @ w_in, offsets, axis=-1)


def mla_queries(q_cols, cos, sin):
    b, l, _ = q_cols.shape
    q = q_cols.reshape(b, l, N_HEADS, QK_HEAD)
    if cos is None:
        return q
    return jnp.concatenate([q[..., :QK_NOPE], apply_rope(q[..., QK_NOPE:], cos, sin)], axis=-1)


def mla_keys_values(ckv, k_rope, kv_norm_g, w_ukv, cos, sin):
    b, l, _ = ckv.shape
    kv = (rms_norm(ckv, kv_norm_g) @ w_ukv).reshape(b, l, N_HEADS, QK_NOPE + V_HEAD)
    k_nope, v = kv[..., :QK_NOPE], kv[..., QK_NOPE:]
    k_rope = k_rope[:, :, None, :]
    if cos is not None:
        k_rope = apply_rope(k_rope, cos, sin)
    k = jnp.concatenate([k_nope, jnp.broadcast_to(k_rope, (b, l, N_HEADS, QK_ROPE))], axis=-1)
    return k, v


def attend(q, k, v):
    b, lq, h, dqk = q.shape
    nb = lq // Q_BLOCK
    qb = (q * (QK_HEAD ** -0.5)).reshape(b, nb, Q_BLOCK, h, dqk).transpose(1, 0, 2, 3, 4)

    def one_block(qi):
        s = jnp.einsum('bqhd,bkhd->bhqk', qi, k).astype(jnp.float32)
        p = jax.nn.softmax(s, axis=-1)
        return jnp.einsum('bhqk,bkhv->bqhv', p.astype(v.dtype), v)

    out = lax.map(one_block, qb)
    return out.transpose(1, 0, 2, 3, 4).reshape(b, lq, h, V_HEAD)


def dwconv(x, w, bias):
    l = x.shape[1]
    xp = jnp.pad(x, ((0, 0), (CONV_PAD_L, CONV_W - 1 - CONV_PAD_L), (0, 0)))
    y = bias
    for k in range(CONV_W):
        y = y + xp[:, k:k + l] * w[k]
    return y


def rglru_coeffs(xr, w_a, b_a, w_i, b_i, lam):
    b, l, _ = xr.shape
    xf = xr.astype(jnp.float32)
    xb = xf.reshape(b, l, RG_BLOCKS, RG_BW)
    r = jax.nn.sigmoid(jnp.einsum('blnj,njk->blnk', xb, w_a.astype(jnp.float32)).reshape(b, l, D_RNN) + b_a)
    i = jax.nn.sigmoid(jnp.einsum('blnj,njk->blnk', xb, w_i.astype(jnp.float32)).reshape(b, l, D_RNN) + b_i)
    log_a = -RG_C * r * jax.nn.softplus(-lam.astype(jnp.float32))
    a = jnp.exp(log_a)
    return a, jnp.sqrt(-jnp.expm1(2.0 * log_a)) * (i * xf)


def linear_scan(a, bx, h0, reverse):
    if h0 is not None:
        first = -1 if reverse else 0
        bx = bx.at[:, first].add(a[:, first] * h0)

    def combine(left, right):
        return left[0] * right[0], right[0] * left[1] + right[1]

    _, h = lax.associative_scan(combine, (a, bx), reverse=reverse, axis=1)
    return h


def gated_merge(o_mla, o_rnn, g_mla, g_rnn, w_o_mla, w_o_rnn, w_out):
    b, l = o_mla.shape[:2]
    y_mla = o_mla.reshape(b, l, N_HEADS * V_HEAD) @ w_o_mla
    y_rnn = o_rnn @ w_o_rnn
    return (jax.nn.sigmoid(g_mla) * y_mla + jax.nn.sigmoid(g_rnn) * y_rnn) @ w_out


def token_mixer(h, hc, w_in, kv_norm_g, w_ukv, conv_w, conv_b, rg_wa, rg_ba, rg_wi, rg_bi, rg_lambda,
                w_o_mla, w_o_rnn, w_out, cos, sin, update_ctx):
    q_l, ckv_l, kr_l, rx_l, ry_l, gm_l, gr_l = combined_projection(h, w_in)
    q_c, ckv_c, kr_c, rx_c, ry_c, gm_c, gr_c = combined_projection(hc, w_in)
    k_l, v_l = mla_keys_values(ckv_l, kr_l, kv_norm_g, w_ukv, cos, sin)
    k_c, v_c = mla_keys_values(ckv_c, kr_c, kv_norm_g, w_ukv, None, None)
    o_l = attend(mla_queries(q_l, cos, sin),
                 jnp.concatenate([k_l, k_c], axis=1), jnp.concatenate([v_l, v_c], axis=1))
    xr_l = dwconv(rx_l, conv_w, conv_b)
    xr_c = dwconv(rx_c, conv_w, conv_b)
    h_l, h_c = [], []
    for d, reverse in enumerate((False, True)):
        a_c, b_c = rglru_coeffs(xr_c, rg_wa[d], rg_ba[d], rg_wi[d], rg_bi[d], rg_lambda[d])
        s_c = linear_scan(a_c, b_c, None, reverse)
        a_l, b_l = rglru_coeffs(xr_l, rg_wa[d], rg_ba[d], rg_wi[d], rg_bi[d], rg_lambda[d])
        h_l.append(linear_scan(a_l, b_l, s_c[:, 0] if reverse else s_c[:, -1], reverse))
        h_c.append(s_c)
    r_l = (h_l[0] + h_l[1]).astype(ry_l.dtype) * jax.nn.gelu(ry_l)
    y_l = gated_merge(o_l, r_l, gm_l, gr_l, w_o_mla, w_o_rnn, w_out)
    if not update_ctx:
        return y_l, None
    o_c = attend(mla_queries(q_c, None, None), k_c, v_c)
    r_c = (h_c[0] + h_c[1]).astype(ry_c.dtype) * jax.nn.gelu(ry_c)
    y_c = gated_merge(o_c, r_c, gm_c, gr_c, w_o_mla, w_o_rnn, w_out)
    return y_l, y_c


def moe_ffn(h, w_router, router_bias, w_gate, w_up, w_down):
    scores = jax.nn.sigmoid((h @ w_router).astype(jnp.float32))
    biased = scores + router_bias.astype(jnp.float32)
    grouped = biased.reshape(biased.shape[:-1] + (N_GROUPS, EXPERTS_PER_GROUP))
    group_score = lax.top_k(grouped, TOP_K)[0].sum(axis=-1)
    best_group = jnp.argmax(group_score, axis=-1)
    in_group = (jnp.arange(N_EXPERTS) // EXPERTS_PER_GROUP) == best_group[..., None]
    _, idx = lax.top_k(jnp.where(in_group, biased, -jnp.inf), TOP_K)
    wsel = jnp.take_along_axis(scores, idx, axis=-1)
    wsel = wsel / jnp.sum(wsel, axis=-1, keepdims=True)
    combine = jnp.einsum('blk,blke->ble', wsel,
                         jax.nn.one_hot(idx, N_EXPERTS, dtype=jnp.float32)).astype(h.dtype)
    out = jnp.zeros_like(h)
    for e in range(N_EXPERTS):
        act = jax.nn.silu(h @ w_gate[e]) * (h @ w_up[e])
        out = out + combine[..., e:e + 1] * (act @ w_down[e])
    return out


def setup_inputs(seed: int = 0) -> dict:
    key = jax.random.key(seed)
    ks = jax.random.split(key, 28)
    f32 = jnp.float32

    def nrm(k, shape, scale):
        return jax.random.normal(k, shape, f32) * scale

    u = jax.random.uniform(ks[17], (DEPTH, 2, D_RNN), f32, minval=0.9, maxval=0.999)
    a_base = u ** (1.0 / RG_C)
    rg_lambda = jnp.log(a_base) - jnp.log1p(-a_base)
    return {
        'x': nrm(ks[0], (BATCH, SEQ, D_MODEL), 1.0),
        'c': nrm(ks[1], (BATCH, D_MODEL), 1.0),
        'ctx': nrm(ks[2], (BATCH, CTX_LEN, D_MODEL), 1.0),
        'c_ctx': nrm(ks[3], (D_MODEL,), 1.0),
        'w_mod': nrm(ks[4], (DEPTH, D_MODEL, N_MOD * D_MODEL), 0.5 * D_MODEL ** -0.5),
        'b_mod': nrm(ks[5], (DEPTH, N_MOD * D_MODEL), 0.01),
        'norm_mix_g': 1.0 + nrm(ks[6], (DEPTH, D_MODEL), 0.02),
        'norm_ffn_g': 1.0 + nrm(ks[7], (DEPTH, D_MODEL), 0.02),
        'w_in': nrm(ks[8], (DEPTH, D_MODEL, IN_WIDTH), D_MODEL ** -0.5),
        'kv_norm_g': 1.0 + nrm(ks[9], (DEPTH, KV_RANK), 0.02),
        'w_ukv': nrm(ks[10], (DEPTH, KV_RANK, N_HEADS * (QK_NOPE + V_HEAD)), KV_RANK ** -0.5),
        'conv_w': nrm(ks[11], (DEPTH, CONV_W, D_RNN), CONV_W ** -0.5),
        'conv_b': nrm(ks[12], (DEPTH, D_RNN), 0.01),
        'rg_wa': nrm(ks[13], (DEPTH, 2, RG_BLOCKS, RG_BW, RG_BW), RG_BW ** -0.5),
        'rg_ba': nrm(ks[14], (DEPTH, 2, D_RNN), 0.01),
        'rg_wi': nrm(ks[15], (DEPTH, 2, RG_BLOCKS, RG_BW, RG_BW), RG_BW ** -0.5),
        'rg_bi': nrm(ks[16], (DEPTH, 2, D_RNN), 0.01),
        'rg_lambda': rg_lambda,
        'w_o_mla': nrm(ks[18], (DEPTH, N_HEADS * V_HEAD, D_MODEL), (N_HEADS * V_HEAD) ** -0.5),
        'w_o_rnn': nrm(ks[19], (DEPTH, D_RNN, D_MODEL), D_RNN ** -0.5),
        'w_out': nrm(ks[20], (DEPTH, D_MODEL, D_MODEL), D_MODEL ** -0.5),
        'w_router': nrm(ks[21], (D_MODEL, N_EXPERTS), D_MODEL ** -0.5),
        'router_bias': nrm(ks[22], (N_EXPERTS,), 0.01),
        'w_gate': nrm(ks[23], (DEPTH, N_EXPERTS, D_MODEL, D_EXPERT), D_MODEL ** -0.5),
        'w_up': nrm(ks[24], (DEPTH, N_EXPERTS, D_MODEL, D_EXPERT), D_MODEL ** -0.5),
        'w_down': nrm(ks[25], (DEPTH, N_EXPERTS, D_EXPERT, D_MODEL), D_EXPERT ** -0.5),
        'final_norm_g': 1.0 + nrm(ks[26], (D_MODEL,), 0.02),
    }


def reference(x, c, ctx, c_ctx, w_mod, b_mod, norm_mix_g, norm_ffn_g, w_in, kv_norm_g, w_ukv,
              conv_w, conv_b, rg_wa, rg_ba, rg_wi, rg_bi, rg_lambda, w_o_mla, w_o_rnn, w_out,
              w_router, router_bias, w_gate, w_up, w_down, final_norm_g):
    cos, sin = rope_tables(x.shape[1])
    silu_c = jax.nn.silu(c)
    silu_cc = jax.nn.silu(c_ctx)
    xc = ctx
    for l in range(DEPTH):
        update_ctx = l < DEPTH - 1
        mod = (silu_c @ w_mod[l] + b_mod[l])[:, None, :]
        mod_c = (silu_cc @ w_mod[l] + b_mod[l])[None, None, :]
        sh1, sc1, g1, sh2, sc2, g2 = jnp.split(mod, N_MOD, axis=-1)
        csh1, csc1, cg1, csh2, csc2, cg2 = jnp.split(mod_c, N_MOD, axis=-1)
        h = modulate(rms_norm(x, norm_mix_g[l]), sh1, sc1)
        hc = modulate(rms_norm(xc, norm_mix_g[l]), csh1, csc1)
        y, yc = token_mixer(h, hc, w_in[l], kv_norm_g[l], w_ukv[l], conv_w[l], conv_b[l],
                            rg_wa[l], rg_ba[l], rg_wi[l], rg_bi[l], rg_lambda[l],
                            w_o_mla[l], w_o_rnn[l], w_out[l], cos, sin, update_ctx)
        x = x + g1 * y
        h = modulate(rms_norm(x, norm_ffn_g[l]), sh2, sc2)
        x = x + g2 * moe_ffn(h, w_router, router_bias, w_gate[l], w_up[l], w_down[l])
        if update_ctx:
            xc = xc + cg1 * yc
            hc = modulate(rms_norm(xc, norm_ffn_g[l]), csh2, csc2)
            xc = xc + cg2 * moe_ffn(hc, w_router, router_bias, w_gate[l], w_up[l], w_down[l])
    return rms_norm(x, final_norm_g)
```

```python
import functools

import jax
import jax.numpy as jnp
from jax import lax
from jax.experimental import pallas as pl
from jax.experimental.pallas import tpu as pltpu

N_HEADS = 16
QK_NOPE = 128
QK_ROPE = 64
QK_HEAD = QK_NOPE + QK_ROPE
V_HEAD = 128
GRID_W = 64
ROPE_THETA = 10000.0
RG_BLOCKS = 8
CONV_W = 4
RG_C = 8.0
N_GROUPS = 4
EPS = 1e-6
N_MOD = 6

LANES = 128
SUBLANES = 8
ROW_BLOCK = 256
VMEM_LIMIT = 56 * 1024 * 1024

F32 = jnp.float32
BF16 = jnp.bfloat16


def _pick(n, candidates):
    for c in candidates:
        if n % c == 0:
            return c
    raise ValueError(f"no tile of {candidates} divides {n}")


def _params(*sem):
    return pltpu.CompilerParams(dimension_semantics=sem, vmem_limit_bytes=VMEM_LIMIT)


def _mod_row(blk, nt, nctx, nb):
    return jnp.where(blk % nt < nctx, nb, blk // nt)


def _rms(x):
    return x * lax.rsqrt(jnp.mean(x * x, axis=-1, keepdims=True) + EPS)


def _mod_kernel(c_ref, w_ref, b_ref, o_ref):
    c = c_ref[...]
    s = (c * jax.nn.sigmoid(c)).astype(BF16)
    o_ref[...] = jnp.dot(s, w_ref[...].astype(BF16), preferred_element_type=F32) + b_ref[...]


def _mod_table(cc, w_mod, b_mod):
    depth, d, n = w_mod.shape
    tn = _pick(n, (1024, 512, 256, 128))
    return pl.pallas_call(
        _mod_kernel,
        grid=(depth, n // tn),
        in_specs=[
            pl.BlockSpec((16, d), lambda l, j: (0, 0)),
            pl.BlockSpec((None, d, tn), lambda l, j: (l, 0, j)),
            pl.BlockSpec((None, 1, tn), lambda l, j: (l, 0, j)),
        ],
        out_specs=pl.BlockSpec((None, 16, tn), lambda l, j: (l, 0, j)),
        out_shape=jax.ShapeDtypeStruct((depth, 16, n), F32),
        compiler_params=_params("arbitrary", "arbitrary"),
        name="mod_table",
    )(cc, w_mod, b_mod.reshape(depth, 1, n))


def _inproj_kernel(x_ref, g_ref, sh_ref, sc_ref, w_ref, o_ref, h_ref, *, tm, nt, nctx, nb):
    i = pl.program_id(0)

    @pl.when(pl.program_id(1) == 0)
    def _():
        for s in range(tm // ROW_BLOCK):
            rows = slice(s * ROW_BLOCK, (s + 1) * ROW_BLOCK)
            row = _mod_row(i * (tm // ROW_BLOCK) + s, nt, nctx, nb)
            y = _rms(x_ref[rows, :]) * g_ref[...]
            y = y * (1.0 + sc_ref[pl.ds(row, 1), :]) + sh_ref[pl.ds(row, 1), :]
            h_ref[rows, :] = y.astype(BF16)

    o_ref[...] = jnp.dot(h_ref[...], w_ref[...], preferred_element_type=F32).astype(BF16)


def _inproj(x2, norm_g, mod_t, w_in_p, l, geo):
    m, d = x2.shape
    n = w_in_p.shape[-1]
    tm = _pick(m, (1024, 512, 256))
    tn = _pick(n, (1024, 512, 256, 128))
    kern = functools.partial(_inproj_kernel, tm=tm, nt=geo["nt"], nctx=geo["nctx"], nb=geo["nb"])
    return pl.pallas_call(
        kern,
        grid=(m // tm, n // tn),
        in_specs=[
            pl.BlockSpec((tm, d), lambda i, j: (i, 0)),
            pl.BlockSpec((None, 1, d), lambda i, j: (l, 0, 0)),
            pl.BlockSpec((None, None, 16, d), lambda i, j: (l, 0, 0, 0)),
            pl.BlockSpec((None, None, 16, d), lambda i, j: (l, 1, 0, 0)),
            pl.BlockSpec((None, d, tn), lambda i, j: (l, 0, j)),
        ],
        out_specs=pl.BlockSpec((tm, tn), lambda i, j: (i, j)),
        out_shape=jax.ShapeDtypeStruct((m, n), BF16),
        scratch_shapes=[pltpu.VMEM((tm, d), BF16)],
        compiler_params=_params("arbitrary", "arbitrary"),
        name="inproj",
    )(x2, norm_g, mod_t, mod_t, w_in_p)


def _kvup_kernel(c_ref, g_ref, w_ref, o_ref, h_ref):
    @pl.when(pl.program_id(1) == 0)
    def _():
        h_ref[...] = (_rms(c_ref[...].astype(F32)) * g_ref[...]).astype(BF16)

    o_ref[...] = jnp.dot(h_ref[...], w_ref[...], preferred_element_type=F32).astype(BF16)


def _kvup(p, kv_g, w_ukv_p, l, off_ckv):
    m = p.shape[0]
    r, n = w_ukv_p.shape[1:]
    tm = _pick(m, (1024, 512, 256))
    tn = _pick(n, (2048, 1024, 512, 256))
    cb = off_ckv // r
    return pl.pallas_call(
        _kvup_kernel,
        grid=(m // tm, n // tn),
        in_specs=[
            pl.BlockSpec((tm, r), lambda i, j: (i, cb)),
            pl.BlockSpec((None, 1, r), lambda i, j: (l, 0, 0)),
            pl.BlockSpec((None, r, tn), lambda i, j: (l, 0, j)),
        ],
        out_specs=pl.BlockSpec((tm, tn), lambda i, j: (i, j)),
        out_shape=jax.ShapeDtypeStruct((m, n), BF16),
        scratch_shapes=[pltpu.VMEM((tm, r), BF16)],
        compiler_params=_params("arbitrary", "arbitrary"),
        name="kvup",
    )(p, kv_g, w_ukv_p)


def _rope(x, cos, sin_signed):
    lane = lax.broadcasted_iota(jnp.int32, x.shape, 1)
    first = (lane % 32) < 16
    rot = jnp.where(first, pltpu.roll(x, LANES - 16, 1), pltpu.roll(x, 16, 1))
    return x * cos + rot * sin_signed


def _attn_kernel(qn_ref, qr_ref, kn_ref, kr_ref, v_ref, cos_ref, sin_ref, o_ref, kcat_ref, *, tq, nctx, lc):
    qi = pl.program_id(2)
    c_rows = nctx * tq

    @pl.when(qi == 0)
    def _():
        kr = _rope(kr_ref[...].astype(F32), cos_ref[...], sin_ref[...])
        for j in range(2):
            kcat_ref[j, :, 0:QK_NOPE] = kn_ref[:, j * QK_NOPE:(j + 1) * QK_NOPE]
            krj = kr if j == 0 else pltpu.roll(kr, QK_ROPE, 1)
            kcat_ref[j, :, QK_NOPE:QK_NOPE + LANES] = krj.astype(BF16)

    scale = QK_HEAD ** -0.5
    r0 = pl.multiple_of(qi * tq, tq)
    qr = _rope(qr_ref[...].astype(F32), cos_ref[pl.ds(r0, tq), :], sin_ref[pl.ds(r0, tq), :])
    qr = (qr * scale).astype(BF16)
    qn = (qn_ref[...].astype(F32) * scale).astype(BF16)

    def attend(nk):
        for j in range(2):
            q = jnp.concatenate([qn[:, j * QK_NOPE:(j + 1) * QK_NOPE], qr], axis=1)
            k = kcat_ref[j, 0:nk, :]
            s = lax.dot_general(q, k, (((1,), (1,)), ((), ())), preferred_element_type=F32)
            p = jnp.exp(s - jnp.max(s, axis=-1, keepdims=True))
            den = jnp.sum(p, axis=-1, keepdims=True)
            o = jnp.dot(p.astype(BF16), v_ref[0:nk, j * V_HEAD:(j + 1) * V_HEAD], preferred_element_type=F32)
            o_ref[:, j * V_HEAD:(j + 1) * V_HEAD] = (o / den).astype(BF16)

    @pl.when(qi < nctx)
    def _():
        attend(c_rows)

    @pl.when(qi >= nctx)
    def _():
        attend(lc)


def _attention(p3, kv3, cos_t, sin_t, geo, off_qr, off_kr):
    b, lc, _ = p3.shape
    hp = N_HEADS // 2
    tq = ROW_BLOCK
    nt = lc // tq
    qrb = off_qr // LANES
    krb = off_kr // LANES
    kern = functools.partial(_attn_kernel, tq=tq, nctx=geo["nctx"], lc=lc)
    return pl.pallas_call(
        kern,
        grid=(b, hp, nt),
        in_specs=[
            pl.BlockSpec((None, tq, 2 * QK_NOPE), lambda bi, h, q: (bi, q, h)),
            pl.BlockSpec((None, tq, LANES), lambda bi, h, q: (bi, q, qrb + h)),
            pl.BlockSpec((None, lc, 2 * QK_NOPE), lambda bi, h, q: (bi, 0, h)),
            pl.BlockSpec((None, lc, LANES), lambda bi, h, q: (bi, 0, krb)),
            pl.BlockSpec((None, lc, 2 * V_HEAD), lambda bi, h, q: (bi, 0, hp + h)),
            pl.BlockSpec((lc, LANES), lambda bi, h, q: (0, 0)),
            pl.BlockSpec((lc, LANES), lambda bi, h, q: (0, 0)),
        ],
        out_specs=pl.BlockSpec((None, tq, 2 * V_HEAD), lambda bi, h, q: (bi, q, h)),
        out_shape=jax.ShapeDtypeStruct((b, lc, N_HEADS * V_HEAD), BF16),
        scratch_shapes=[pltpu.VMEM((2, lc, QK_NOPE + LANES), BF16)],
        compiler_params=_params("arbitrary", "arbitrary", "arbitrary"),
        name="attention",
    )(p3, p3, kv3, p3, kv3, cos_t, sin_t)


def _scan_tile(a3, b3, carry, reverse):
    sub = lax.broadcasted_iota(jnp.int32, a3.shape, 1)
    for k in (1, 2, 4):
        if reverse:
            keep = sub < SUBLANES - k
            shift = SUBLANES - k
        else:
            keep = sub >= k
            shift = k
        a_s = jnp.where(keep, pltpu.roll(a3, shift, 1), 1.0)
        b_s = jnp.where(keep, pltpu.roll(b3, shift, 1), 0.0)
        b3 = b3 + a3 * b_s
        a3 = a3 * a_s
    return a3, b3


def _rglru_kernel(x_ref, xp_ref, xn_ref, cw_ref, cb_ref, w_ref, rp_ref, o_ref, a_ref, b_ref, h_ref,
                  *, tt, nt, nctx, d, bw):
    dr = pl.program_id(0)
    i = pl.program_id(2)
    t = _tile_of(dr, i, nt, nctx)
    first = jnp.logical_or(t == 0, t == nctx)
    last = jnp.logical_or(t == nctx - 1, t == nt - 1)

    @pl.when(i == 0)
    def _():
        h_ref[...] = jnp.zeros_like(h_ref)

    x = x_ref[...].astype(F32)
    prev = jnp.where(first, 0.0, xp_ref[...].astype(F32)[SUBLANES:, :])
    nxt = jnp.where(last, 0.0, xn_ref[...].astype(F32)[:SUBLANES, :])
    xe = jnp.concatenate([prev, x, nxt], axis=0)
    xr = cb_ref[...]
    for k in range(CONV_W):
        xr = xr + xe[SUBLANES - 2 + k:SUBLANES - 2 + k + tt, :] * cw_ref[k:k + 1, :]
    xb = xr.astype(BF16)

    ba = rp_ref[0:1, :]
    bi = rp_ref[1:2, :]
    sp = jax.nn.softplus(-rp_ref[2:3, :])
    for n in range(d // bw):
        cols = slice(n * bw, (n + 1) * bw)
        g = jnp.dot(xb[:, cols], w_ref[n], preferred_element_type=F32)
        r = jax.nn.sigmoid(g[:, :bw] + ba[:, cols])
        ig = jax.nn.sigmoid(g[:, bw:] + bi[:, cols])
        log_a = (-RG_C) * r * sp[:, cols]
        a = jnp.exp(log_a)
        bx = jnp.sqrt(1.0 - a * a) * (ig * xr[:, cols])
        a_ref[:, :, cols] = a.reshape(tt // SUBLANES, SUBLANES, bw)
        b_ref[:, :, cols] = bx.reshape(tt // SUBLANES, SUBLANES, bw)

    ng = tt // SUBLANES

    def run(reverse):
        a3, b3 = _scan_tile(a_ref[...], b_ref[...], None, reverse)
        a_ref[...] = a3
        b_ref[...] = b3

        def body(s, carry):
            g = (ng - 1 - s) if reverse else s
            hg = b_ref[g] + a_ref[g] * carry
            b_ref[g] = hg
            edge = hg[0:1, :] if reverse else hg[SUBLANES - 1:SUBLANES, :]
            return jnp.broadcast_to(edge, hg.shape)

        h_ref[...] = lax.fori_loop(0, ng, body, h_ref[...])

    @pl.when(dr == 0)
    def _():
        run(False)

    @pl.when(dr == 1)
    def _():
        run(True)

    o_ref[...] = b_ref[...].reshape(tt, d).astype(BF16)


def _tile_of(dr, i, nt, nctx):
    back = jnp.where(i < nctx, nctx - 1 - i, nt - 1 - (i - nctx))
    return jnp.where(dr == 0, i, back)


def _rglru(p3, conv_w, conv_b, rg_w, rg_p, l, geo, off_rx):
    b, lc, _ = p3.shape
    d = conv_w.shape[-1]
    bw = d // RG_BLOCKS
    tt = ROW_BLOCK
    nt = lc // tt
    nctx = geo["nctx"]
    xb = off_rx // d
    hb = tt // 16
    nh = lc // 16

    def tile(dr, i):
        return _tile_of(dr, i, nt, nctx)

    kern = functools.partial(_rglru_kernel, tt=tt, nt=nt, nctx=nctx, d=d, bw=bw)
    return pl.pallas_call(
        kern,
        grid=(2, b, nt),
        in_specs=[
            pl.BlockSpec((None, tt, d), lambda dr, bi, i: (bi, tile(dr, i), xb)),
            pl.BlockSpec((None, 16, d), lambda dr, bi, i: (bi, jnp.maximum(tile(dr, i) * hb - 1, 0), xb)),
            pl.BlockSpec((None, 16, d), lambda dr, bi, i: (bi, jnp.minimum((tile(dr, i) + 1) * hb, nh - 1), xb)),
            pl.BlockSpec((None, CONV_W, d), lambda dr, bi, i: (l, 0, 0)),
            pl.BlockSpec((None, 1, d), lambda dr, bi, i: (l, 0, 0)),
            pl.BlockSpec((None, None, RG_BLOCKS, bw, 2 * bw), lambda dr, bi, i: (l, dr, 0, 0, 0)),
            pl.BlockSpec((None, None, 3, d), lambda dr, bi, i: (l, dr, 0, 0)),
        ],
        out_specs=pl.BlockSpec((None, None, tt, d), lambda dr, bi, i: (dr, bi, tile(dr, i), 0)),
        out_shape=jax.ShapeDtypeStruct((2, b, lc, d), BF16),
        scratch_shapes=[
            pltpu.VMEM((tt // SUBLANES, SUBLANES, d), F32),
            pltpu.VMEM((tt // SUBLANES, SUBLANES, d), F32),
            pltpu.VMEM((SUBLANES, d), F32),
        ],
        compiler_params=_params("arbitrary", "arbitrary", "arbitrary"),
        name="rglru",
    )(p3, p3, p3, conv_w, conv_b, rg_w, rg_p)


def _merge_kernel(o_ref, hf_ref, hb_ref, ry_ref, gm_ref, gr_ref, wm_ref, wr_ref, z_ref, r_ref):
    @pl.when(pl.program_id(1) == 0)
    def _():
        h = hf_ref[...].astype(F32) + hb_ref[...].astype(F32)
        r_ref[...] = (h * jax.nn.gelu(ry_ref[...].astype(F32))).astype(BF16)

    ym = jnp.dot(o_ref[...], wm_ref[...], preferred_element_type=F32)
    yr = jnp.dot(r_ref[...], wr_ref[...], preferred_element_type=F32)
    z = jax.nn.sigmoid(gm_ref[...].astype(F32)) * ym + jax.nn.sigmoid(gr_ref[...].astype(F32)) * yr
    z_ref[...] = z.astype(BF16)


def _merge(o2, h2d, p, w_o_mla, w_o_rnn, l, off_ry, off_gm, off_gr):
    m, d = o2.shape
    tm = _pick(m, (512, 256))
    tn = _pick(d, (512, 256))
    ryb = off_ry // d
    gmb = off_gm // tn
    grb = off_gr // tn
    return pl.pallas_call(
        _merge_kernel,
        grid=(m // tm, d // tn),
        in_specs=[
            pl.BlockSpec((tm, d), lambda i, j: (i, 0)),
            pl.BlockSpec((None, tm, d), lambda i, j: (0, i, 0)),
            pl.BlockSpec((None, tm, d), lambda i, j: (1, i, 0)),
            pl.BlockSpec((tm, d), lambda i, j: (i, ryb)),
            pl.BlockSpec((tm, tn), lambda i, j: (i, gmb + j)),
            pl.BlockSpec((tm, tn), lambda i, j: (i, grb + j)),
            pl.BlockSpec((None, d, tn), lambda i, j: (l, 0, j)),
            pl.BlockSpec((None, d, tn), lambda i, j: (l, 0, j)),
        ],
        out_specs=pl.BlockSpec((tm, tn), lambda i, j: (i, j)),
        out_shape=jax.ShapeDtypeStruct((m, d), BF16),
        scratch_shapes=[pltpu.VMEM((tm, d), BF16)],
        compiler_params=_params("arbitrary", "arbitrary"),
        name="merge",
    )(o2, h2d, h2d, p, p, p, w_o_mla, w_o_rnn)


def _outproj_kernel(z_ref, w_ref, x_ref, g_ref, o_ref, *, tm, nt, nctx, nb):
    i = pl.program_id(0)
    y = jnp.dot(z_ref[...], w_ref[...], preferred_element_type=F32)
    for s in range(tm // ROW_BLOCK):
        rows = slice(s * ROW_BLOCK, (s + 1) * ROW_BLOCK)
        row = _mod_row(i * (tm // ROW_BLOCK) + s, nt, nctx, nb)
        o_ref[rows, :] = x_ref[rows, :] + g_ref[pl.ds(row, 1), :] * y[rows, :]


def _outproj(z, w_out, x2, mod_t, l, geo):
    m, d = x2.shape
    tm = _pick(m, (1024, 512, 256))
    tn = _pick(d, (512, 256))
    kern = functools.partial(_outproj_kernel, tm=tm, nt=geo["nt"], nctx=geo["nctx"], nb=geo["nb"])
    return pl.pallas_call(
        kern,
        grid=(m // tm, d // tn),
        in_specs=[
            pl.BlockSpec((tm, d), lambda i, j: (i, 0)),
            pl.BlockSpec((None, d, tn), lambda i, j: (l, 0, j)),
            pl.BlockSpec((tm, tn), lambda i, j: (i, j)),
            pl.BlockSpec((None, None, 16, tn), lambda i, j: (l, 2, 0, j)),
        ],
        out_specs=pl.BlockSpec((tm, tn), lambda i, j: (i, j)),
        out_shape=jax.ShapeDtypeStruct((m, d), F32),
        input_output_aliases={2: 0},
        compiler_params=_params("arbitrary", "arbitrary"),
        name="outproj",
    )(z, w_out, x2, mod_t)


def _router(h, wr_hi, wr_lo, rb):
    hi = h.astype(BF16)
    lo = (h - hi.astype(F32)).astype(BF16)
    nt_dims = (((1,), (1,)), ((), ()))
    logits = (lax.dot_general(wr_hi, hi, nt_dims, preferred_element_type=F32)
              + lax.dot_general(wr_hi, lo, nt_dims, preferred_element_type=F32)
              + lax.dot_general(wr_lo, hi, nt_dims, preferred_element_type=F32))
    scores = jax.nn.sigmoid(logits)
    biased = scores + rb
    e = scores.shape[0]
    per = e // N_GROUPS
    rows_b = [biased[j:j + 1, :] for j in range(e)]
    rows_s = [scores[j:j + 1, :] for j in range(e)]
    gscore = []
    for g in range(N_GROUPS):
        r = rows_b[g * per:(g + 1) * per]
        best = None
        for a in range(per):
            for c in range(a + 1, per):
                pair = r[a] + r[c]
                best = pair if best is None else jnp.maximum(best, pair)
        gscore.append(best)
    gbest = gscore[0]
    gidx = jnp.zeros_like(gbest, dtype=jnp.int32)
    for g in range(1, N_GROUPS):
        better = gscore[g] > gbest
        gbest = jnp.where(better, gscore[g], gbest)
        gidx = jnp.where(better, g, gidx)
    sel = []
    for g in range(N_GROUPS):
        r = rows_b[g * per:(g + 1) * per]
        for a in range(per):
            rank = jnp.zeros_like(gidx)
            for c in range(per):
                if c == a:
                    continue
                ahead = (r[c] > r[a]) if c > a else (r[c] >= r[a])
                rank = rank + ahead.astype(jnp.int32)
            sel.append(jnp.logical_and(gidx == g, rank < 2))
    den = None
    for j in range(e):
        term = jnp.where(sel[j], rows_s[j], 0.0)
        den = term if den is None else den + term
    return [jnp.where(sel[j], rows_s[j] / den, 0.0) for j in range(e)]


def _ffnprep_kernel(x_ref, g_ref, sh_ref, sc_ref, wh_ref, wl_ref, rb_ref, h_ref, cw_ref, *, tm, nt, nctx, nb):
    i = pl.program_id(0)
    hs = []
    for s in range(tm // ROW_BLOCK):
        rows = slice(s * ROW_BLOCK, (s + 1) * ROW_BLOCK)
        row = _mod_row(i * (tm // ROW_BLOCK) + s, nt, nctx, nb)
        y = _rms(x_ref[rows, :]) * g_ref[...]
        y = y * (1.0 + sc_ref[pl.ds(row, 1), :]) + sh_ref[pl.ds(row, 1), :]
        h_ref[rows, :] = y.astype(BF16)
        hs.append(y)
    h = jnp.concatenate(hs, axis=0) if len(hs) > 1 else hs[0]
    comb = _router(h, wh_ref[...], wl_ref[...], rb_ref[...])
    pad = jnp.zeros((LANES - len(comb), tm), F32)
    cw_ref[...] = jnp.transpose(jnp.concatenate(comb + [pad], axis=0))


def _ffnprep(x2, norm_g, mod_t, wr_hi, wr_lo, rbias, l, geo):
    m, d = x2.shape
    e = wr_hi.shape[0]
    tm = _pick(m, (512, 256))
    kern = functools.partial(_ffnprep_kernel, tm=tm, nt=geo["nt"], nctx=geo["nctx"], nb=geo["nb"])
    return pl.pallas_call(
        kern,
        grid=(m // tm,),
        in_specs=[
            pl.BlockSpec((tm, d), lambda i: (i, 0)),
            pl.BlockSpec((None, 1, d), lambda i: (l, 0, 0)),
            pl.BlockSpec((None, None, 16, d), lambda i: (l, 3, 0, 0)),
            pl.BlockSpec((None, None, 16, d), lambda i: (l, 4, 0, 0)),
            pl.BlockSpec((e, d), lambda i: (0, 0)),
            pl.BlockSpec((e, d), lambda i: (0, 0)),
            pl.BlockSpec((e, 1), lambda i: (0, 0)),
        ],
        out_specs=[
            pl.BlockSpec((tm, d), lambda i: (i, 0)),
            pl.BlockSpec((tm, LANES), lambda i: (i, 0)),
        ],
        out_shape=[jax.ShapeDtypeStruct((m, d), BF16), jax.ShapeDtypeStruct((m, LANES), F32)],
        compiler_params=_params("arbitrary"),
        name="ffnprep",
    )(x2, norm_g, mod_t, mod_t, wr_hi, wr_lo, rbias)


def _moe_kernel(h_ref, cw_ref, wg_ref, wu_ref, wd_ref, acc_ref, o_ref):
    e = pl.program_id(0)
    h = h_ref[...]
    gate = jnp.dot(h, wg_ref[...], preferred_element_type=F32)
    up = jnp.dot(h, wu_ref[...], preferred_element_type=F32)
    act = (gate * jax.nn.sigmoid(gate) * up).astype(BF16)
    y = jnp.dot(act, wd_ref[...], preferred_element_type=F32)
    cw = cw_ref[...]
    lane = lax.broadcasted_iota(jnp.int32, cw.shape, 1)
    c = jnp.sum(jnp.where(lane == e, cw, 0.0), axis=1, keepdims=True)
    y = c * y

    @pl.when(e == 0)
    def _():
        o_ref[...] = y

    @pl.when(e > 0)
    def _():
        o_ref[...] = acc_ref[...] + y


def _moe(h2, cw, w_gate, w_up, w_down, acc, l):
    m, d = h2.shape
    ne, _, f = w_gate.shape[1:]
    tm = ROW_BLOCK
    return pl.pallas_call(
        _moe_kernel,
        grid=(ne, m // tm),
        in_specs=[
            pl.BlockSpec((tm, d), lambda e, i: (i, 0)),
            pl.BlockSpec((tm, LANES), lambda e, i: (i, 0)),
            pl.BlockSpec((None, None, d, f), lambda e, i: (l, e, 0, 0)),
            pl.BlockSpec((None, None, d, f), lambda e, i: (l, e, 0, 0)),
            pl.BlockSpec((None, None, f, d), lambda e, i: (l, e, 0, 0)),
            pl.BlockSpec((tm, d), lambda e, i: (i, 0)),
        ],
        out_specs=pl.BlockSpec((tm, d), lambda e, i: (i, 0)),
        out_shape=jax.ShapeDtypeStruct((m, d), F32),
        input_output_aliases={5: 0},
        compiler_params=_params("arbitrary", "arbitrary"),
        name="moe",
    )(h2, cw, w_gate, w_up, w_down, acc)


def _resid_kernel(x_ref, y_ref, g_ref, o_ref, *, tm, nt, nctx, nb):
    i = pl.program_id(0)
    for s in range(tm // ROW_BLOCK):
        rows = slice(s * ROW_BLOCK, (s + 1) * ROW_BLOCK)
        row = _mod_row(i * (tm // ROW_BLOCK) + s, nt, nctx, nb)
        o_ref[rows, :] = x_ref[rows, :] + g_ref[pl.ds(row, 1), :] * y_ref[rows, :]


def _resid(x2, y, mod_t, l, geo):
    m, d = x2.shape
    tm = _pick(m, (512, 256))
    kern = functools.partial(_resid_kernel, tm=tm, nt=geo["nt"], nctx=geo["nctx"], nb=geo["nb"])
    return pl.pallas_call(
        kern,
        grid=(m // tm,),
        in_specs=[
            pl.BlockSpec((tm, d), lambda i: (i, 0)),
            pl.BlockSpec((tm, d), lambda i: (i, 0)),
            pl.BlockSpec((None, None, 16, d), lambda i: (l, 5, 0, 0)),
        ],
        out_specs=pl.BlockSpec((tm, d), lambda i: (i, 0)),
        out_shape=jax.ShapeDtypeStruct((m, d), F32),
        input_output_aliases={0: 0},
        compiler_params=_params("arbitrary"),
        name="resid",
    )(x2, y, mod_t)


def _final_kernel(x_ref, g_ref, o_ref):
    o_ref[...] = _rms(x_ref[...]) * g_ref[...]


def _final_norm(x3, g, c_rows, s_rows):
    b, lc, d = x3.shape
    tm = ROW_BLOCK
    off = c_rows // tm
    return pl.pallas_call(
        _final_kernel,
        grid=(b, s_rows // tm),
        in_specs=[
            pl.BlockSpec((None, tm, d), lambda bi, i: (bi, off + i, 0)),
            pl.BlockSpec((1, d), lambda bi, i: (0, 0)),
        ],
        out_specs=pl.BlockSpec((None, tm, d), lambda bi, i: (bi, i, 0)),
        out_shape=jax.ShapeDtypeStruct((b, s_rows, d), F32),
        compiler_params=_params("arbitrary", "arbitrary"),
        name="final_norm",
    )(x3, g.reshape(1, d))


def _rope_tables(c_rows, s_rows):
    rows = s_rows // GRID_W
    row = jnp.broadcast_to(jnp.arange(rows)[:, None], (rows, GRID_W)).reshape(-1).astype(F32)
    col = jnp.broadcast_to(jnp.arange(GRID_W)[None, :], (rows, GRID_W)).reshape(-1).astype(F32)
    half = QK_ROPE // 2
    inv = ROPE_THETA ** (-jnp.arange(0, half, 2, dtype=F32) / half)
    ang_r = row[:, None] * inv
    ang_c = col[:, None] * inv
    ang = jnp.concatenate([ang_r, ang_r, ang_c, ang_c], axis=-1)
    ang = jnp.concatenate([jnp.zeros((c_rows, QK_ROPE), F32), ang], axis=0)
    ang = jnp.concatenate([ang, ang], axis=-1)
    lane = jnp.arange(LANES)
    sign = jnp.where((lane % 32) < 16, -1.0, 1.0).astype(F32)
    return jnp.cos(ang), jnp.sin(ang) * sign


def kernel(x, c, ctx, c_ctx, w_mod, b_mod, norm_mix_g, norm_ffn_g, w_in, kv_norm_g, w_ukv, conv_w, conv_b,
           rg_wa, rg_ba, rg_wi, rg_bi, rg_lambda, w_o_mla, w_o_rnn, w_out, w_router, router_bias,
           w_gate, w_up, w_down, final_norm_g):
    b, s_rows, d = x.shape
    c_rows = ctx.shape[1]
    depth = w_mod.shape[0]
    kv_rank = kv_norm_g.shape[-1]
    lc = c_rows + s_rows
    m = b * lc
    assert c_rows % ROW_BLOCK == 0 and s_rows % ROW_BLOCK == 0 and b < 16
    geo = {"nt": lc // ROW_BLOCK, "nctx": c_rows // ROW_BLOCK, "nb": b}

    nq = N_HEADS * QK_HEAD
    wq = w_in[:, :, :nq].reshape(depth, d, N_HEADS, QK_HEAD)
    off_qr = N_HEADS * QK_NOPE
    off_ckv = off_qr + N_HEADS * QK_ROPE
    off_kr = off_ckv + kv_rank
    off_rx = -(-(off_kr + LANES) // d) * d
    off_ry, off_gm, off_gr = off_rx + d, off_rx + 2 * d, off_rx + 3 * d
    tail = nq + kv_rank + QK_ROPE
    w_in_p = jnp.concatenate([
        wq[..., :QK_NOPE].reshape(depth, d, -1),
        wq[..., QK_NOPE:].reshape(depth, d, -1),
        w_in[:, :, nq:tail],
        jnp.zeros((depth, d, off_rx - off_kr - QK_ROPE), w_in.dtype),
        w_in[:, :, tail:],
    ], axis=-1).astype(BF16)
    wkv = w_ukv.reshape(depth, kv_rank, N_HEADS, QK_NOPE + V_HEAD)
    w_ukv_p = jnp.concatenate([wkv[..., :QK_NOPE].reshape(depth, kv_rank, -1),
                               wkv[..., QK_NOPE:].reshape(depth, kv_rank, -1)], axis=-1).astype(BF16)
    rg_w = jnp.concatenate([rg_wa, rg_wi], axis=-1).astype(BF16)
    rg_p = jnp.stack([rg_ba, rg_bi, rg_lambda], axis=2)
    w_o_mla_b, w_o_rnn_b, w_out_b = w_o_mla.astype(BF16), w_o_rnn.astype(BF16), w_out.astype(BF16)
    w_gate_b, w_up_b, w_down_b = w_gate.astype(BF16), w_up.astype(BF16), w_down.astype(BF16)
    wr_t = w_router.T
    wr_hi = wr_t.astype(BF16)
    wr_lo = (wr_t - wr_hi.astype(F32)).astype(BF16)
    rbias = router_bias.reshape(-1, 1).astype(F32)
    norm_mix = norm_mix_g.reshape(depth, 1, d)
    norm_ffn = norm_ffn_g.reshape(depth, 1, d)
    kv_g = kv_norm_g.reshape(depth, 1, kv_rank)
    conv_b3 = conv_b.reshape(depth, 1, d)
    cos_t, sin_t = _rope_tables(c_rows, s_rows)

    cc = jnp.concatenate([c, c_ctx[None, :], jnp.zeros((16 - b - 1, d), F32)], axis=0)
    mod_t = _mod_table(cc, w_mod, b_mod).reshape(depth, 16, N_MOD, d).transpose(0, 2, 1, 3)

    x2 = jnp.concatenate([ctx, x], axis=1).reshape(m, d)
    for l in range(depth):
        p = _inproj(x2, norm_mix, mod_t, w_in_p, l, geo)
        p3 = p.reshape(b, lc, -1)
        kv = _kvup(p, kv_g, w_ukv_p, l, off_ckv)
        o = _attention(p3, kv.reshape(b, lc, -1), cos_t, sin_t, geo, off_qr, off_kr)
        hd = _rglru(p3, conv_w, conv_b3, rg_w, rg_p, l, geo, off_rx)
        z = _merge(o.reshape(m, -1), hd.reshape(2, m, d), p, w_o_mla_b, w_o_rnn_b, l, off_ry, off_gm, off_gr)
        x2 = _outproj(z, w_out_b, x2, mod_t, l, geo)
        h2, cw = _ffnprep(x2, norm_ffn, mod_t, wr_hi, wr_lo, rbias, l, geo)
        y = _moe(h2, cw, w_gate_b, w_up_b, w_down_b, jnp.zeros((m, d), F32), l)
        x2 = _resid(x2, y, mod_t, l, geo)
    return _final_norm(x2.reshape(b, lc, d), final_norm_g, c_rows, s_rows)
```

```python
import functools

import jax
import jax.numpy as jnp
from jax import lax
from jax.experimental import pallas as pl
from jax.experimental.pallas import tpu as pltpu

N_HEADS = 16
QK_NOPE = 128
QK_ROPE = 64
QK_HEAD = QK_NOPE + QK_ROPE
V_HEAD = 128
GRID_W = 64
ROPE_THETA = 10000.0
RG_BLOCKS = 8
CONV_W = 4
RG_C = 8.0
N_GROUPS = 4
EPS = 1e-6
N_MOD = 6

LANES = 128
SUBLANES = 8
ROW_BLOCK = 256
VMEM_LIMIT = 56 * 1024 * 1024

F32 = jnp.float32
BF16 = jnp.bfloat16


def _pick(n, candidates):
    for c in candidates:
        if n % c == 0:
            return c
    raise ValueError(f"no tile of {candidates} divides {n}")


def _params(*sem):
    return pltpu.CompilerParams(dimension_semantics=sem, vmem_limit_bytes=VMEM_LIMIT)


def _mod_row(blk, nt, nctx, nb):
    return jnp.where(blk % nt < nctx, nb, blk // nt)


def _rms(x):
    return x * lax.rsqrt(jnp.mean(x * x, axis=-1, keepdims=True) + EPS)


def _mod_kernel(c_ref, w_ref, b_ref, o_ref):
    c = c_ref[...]
    s = (c * jax.nn.sigmoid(c)).astype(BF16)
    o_ref[...] = jnp.dot(s, w_ref[...].astype(BF16), preferred_element_type=F32) + b_ref[...]


def _mod_table(cc, w_mod, b_mod):
    depth, d, n = w_mod.shape
    tn = _pick(n, (1024, 512, 256, 128))
    return pl.pallas_call(
        _mod_kernel,
        grid=(depth, n // tn),
        in_specs=[
            pl.BlockSpec((16, d), lambda l, j: (0, 0)),
            pl.BlockSpec((None, d, tn), lambda l, j: (l, 0, j)),
            pl.BlockSpec((None, 1, tn), lambda l, j: (l, 0, j)),
        ],
        out_specs=pl.BlockSpec((None, 16, tn), lambda l, j: (l, 0, j)),
        out_shape=jax.ShapeDtypeStruct((depth, 16, n), F32),
        compiler_params=_params("arbitrary", "arbitrary"),
        name="mod_table",
    )(cc, w_mod, b_mod.reshape(depth, 1, n))


def _inproj_kernel(x_ref, g_ref, sh_ref, sc_ref, w_ref, o_ref, h_ref, *, tm, nt, nctx, nb):
    i = pl.program_id(0)

    @pl.when(pl.program_id(1) == 0)
    def _():
        for s in range(tm // ROW_BLOCK):
            rows = slice(s * ROW_BLOCK, (s + 1) * ROW_BLOCK)
            row = _mod_row(i * (tm // ROW_BLOCK) + s, nt, nctx, nb)
            y = _rms(x_ref[rows, :]) * g_ref[...]
            y = y * (1.0 + sc_ref[pl.ds(row, 1), :]) + sh_ref[pl.ds(row, 1), :]
            h_ref[rows, :] = y.astype(BF16)

    o_ref[...] = jnp.dot(h_ref[...], w_ref[...], preferred_element_type=F32).astype(BF16)


def _inproj(x2, norm_g, mod_t, w_in_p, l, geo):
    m, d = x2.shape
    n = w_in_p.shape[-1]
    tm = _pick(m, (1024, 512, 256))
    tn = _pick(n, (1024, 512, 256, 128))
    kern = functools.partial(_inproj_kernel, tm=tm, nt=geo["nt"], nctx=geo["nctx"], nb=geo["nb"])
    return pl.pallas_call(
        kern,
        grid=(m // tm, n // tn),
        in_specs=[
            pl.BlockSpec((tm, d), lambda i, j: (i, 0)),
            pl.BlockSpec((None, 1, d), lambda i, j: (l, 0, 0)),
            pl.BlockSpec((None, None, 16, d), lambda i, j: (l, 0, 0, 0)),
            pl.BlockSpec((None, None, 16, d), lambda i, j: (l, 1, 0, 0)),
            pl.BlockSpec((None, d, tn), lambda i, j: (l, 0, j)),
        ],
        out_specs=pl.BlockSpec((tm, tn), lambda i, j: (i, j)),
        out_shape=jax.ShapeDtypeStruct((m, n), BF16),
        scratch_shapes=[pltpu.VMEM((tm, d), BF16)],
        compiler_params=_params("arbitrary", "arbitrary"),
        name="inproj",
    )(x2, norm_g, mod_t, mod_t, w_in_p)


def _kvup_kernel(c_ref, g_ref, w_ref, o_ref, h_ref):
    @pl.when(pl.program_id(1) == 0)
    def _():
        h_ref[...] = (_rms(c_ref[...].astype(F32)) * g_ref[...]).astype(BF16)

    o_ref[...] = jnp.dot(h_ref[...], w_ref[...], preferred_element_type=F32).astype(BF16)


def _kvup(p, kv_g, w_ukv_p, l, off_ckv):
    m = p.shape[0]
    r, n = w_ukv_p.shape[1:]
    tm = _pick(m, (1024, 512, 256))
    tn = _pick(n, (2048, 1024, 512, 256))
    cb = off_ckv // r
    return pl.pallas_call(
        _kvup_kernel,
        grid=(m // tm, n // tn),
        in_specs=[
            pl.BlockSpec((tm, r), lambda i, j: (i, cb)),
            pl.BlockSpec((None, 1, r), lambda i, j: (l, 0, 0)),
            pl.BlockSpec((None, r, tn), lambda i, j: (l, 0, j)),
        ],
        out_specs=pl.BlockSpec((tm, tn), lambda i, j: (i, j)),
        out_shape=jax.ShapeDtypeStruct((m, n), BF16),
        scratch_shapes=[pltpu.VMEM((tm, r), BF16)],
        compiler_params=_params("arbitrary", "arbitrary"),
        name="kvup",
    )(p, kv_g, w_ukv_p)


def _rope(x, cos, sin_signed):
    lane = lax.broadcasted_iota(jnp.int32, x.shape, 1)
    first = (lane % 32) < 16
    rot = jnp.where(first, pltpu.roll(x, LANES - 16, 1), pltpu.roll(x, 16, 1))
    return x * cos + rot * sin_signed


def _attn_kernel(qn_ref, qr_ref, kn_ref, kr_ref, v_ref, cos_ref, sin_ref, o_ref, kcat_ref, *, tq, nctx, lc):
    qi = pl.program_id(2)
    c_rows = nctx * tq

    @pl.when(qi == 0)
    def _():
        kr = _rope(kr_ref[...].astype(F32), cos_ref[...], sin_ref[...])
        for j in range(2):
            kcat_ref[j, :, 0:QK_NOPE] = kn_ref[:, j * QK_NOPE:(j + 1) * QK_NOPE]
            krj = kr if j == 0 else pltpu.roll(kr, QK_ROPE, 1)
            kcat_ref[j, :, QK_NOPE:QK_NOPE + LANES] = krj.astype(BF16)

    scale = QK_HEAD ** -0.5
    r0 = pl.multiple_of(qi * tq, tq)
    qr = _rope(qr_ref[...].astype(F32), cos_ref[pl.ds(r0, tq), :], sin_ref[pl.ds(r0, tq), :])
    qr = (qr * scale).astype(BF16)
    qn = (qn_ref[...].astype(F32) * scale).astype(BF16)

    def attend(nk):
        for j in range(2):
            q = jnp.concatenate([qn[:, j * QK_NOPE:(j + 1) * QK_NOPE], qr], axis=1)
            k = kcat_ref[j, 0:nk, :]
            s = lax.dot_general(q, k, (((1,), (1,)), ((), ())), preferred_element_type=F32)
            p = jnp.exp(s - jnp.max(s, axis=-1, keepdims=True))
            den = jnp.sum(p, axis=-1, keepdims=True)
            o = jnp.dot(p.astype(BF16), v_ref[0:nk, j * V_HEAD:(j + 1) * V_HEAD], preferred_element_type=F32)
            o_ref[:, j * V_HEAD:(j + 1) * V_HEAD] = (o / den).astype(BF16)

    @pl.when(qi < nctx)
    def _():
        attend(c_rows)

    @pl.when(qi >= nctx)
    def _():
        attend(lc)


def _attention(p3, kv3, cos_t, sin_t, geo, off_qr, off_kr):
    b, lc, _ = p3.shape
    hp = N_HEADS // 2
    tq = ROW_BLOCK
    nt = lc // tq
    qrb = off_qr // LANES
    krb = off_kr // LANES
    kern = functools.partial(_attn_kernel, tq=tq, nctx=geo["nctx"], lc=lc)
    return pl.pallas_call(
        kern,
        grid=(b, hp, nt),
        in_specs=[
            pl.BlockSpec((None, tq, 2 * QK_NOPE), lambda bi, h, q: (bi, q, h)),
            pl.BlockSpec((None, tq, LANES), lambda bi, h, q: (bi, q, qrb + h)),
            pl.BlockSpec((None, lc, 2 * QK_NOPE), lambda bi, h, q: (bi, 0, h)),
            pl.BlockSpec((None, lc, LANES), lambda bi, h, q: (bi, 0, krb)),
            pl.BlockSpec((None, lc, 2 * V_HEAD), lambda bi, h, q: (bi, 0, hp + h)),
            pl.BlockSpec((lc, LANES), lambda bi, h, q: (0, 0)),
            pl.BlockSpec((lc, LANES), lambda bi, h, q: (0, 0)),
        ],
        out_specs=pl.BlockSpec((None, tq, 2 * V_HEAD), lambda bi, h, q: (bi, q, h)),
        out_shape=jax.ShapeDtypeStruct((b, lc, N_HEADS * V_HEAD), BF16),
        scratch_shapes=[pltpu.VMEM((2, lc, QK_NOPE + LANES), BF16)],
        compiler_params=_params("arbitrary", "arbitrary", "arbitrary"),
        name="attention",
    )(p3, p3, kv3, p3, kv3, cos_t, sin_t)


def _scan_tile(a3, b3, carry, reverse):
    sub = lax.broadcasted_iota(jnp.int32, a3.shape, 1)
    for k in (1, 2, 4):
        if reverse:
            keep = sub < SUBLANES - k
            shift = SUBLANES - k
        else:
            keep = sub >= k
            shift = k
        a_s = jnp.where(keep, pltpu.roll(a3, shift, 1), 1.0)
        b_s = jnp.where(keep, pltpu.roll(b3, shift, 1), 0.0)
        b3 = b3 + a3 * b_s
        a3 = a3 * a_s
    return a3, b3


def _rglru_kernel(x_ref, xp_ref, xn_ref, cw_ref, cb_ref, w_ref, rp_ref, o_ref, a_ref, b_ref, h_ref,
                  *, tt, nt, nctx, d, bw):
    dr = pl.program_id(0)
    i = pl.program_id(2)
    t = _tile_of(dr, i, nt, nctx)
    first = jnp.logical_or(t == 0, t == nctx)
    last = jnp.logical_or(t == nctx - 1, t == nt - 1)

    @pl.when(i == 0)
    def _():
        h_ref[...] = jnp.zeros_like(h_ref)

    x = x_ref[...].astype(F32)
    prev = jnp.where(first, 0.0, xp_ref[...].astype(F32)[SUBLANES:, :])
    nxt = jnp.where(last, 0.0, xn_ref[...].astype(F32)[:SUBLANES, :])
    xe = jnp.concatenate([prev, x, nxt], axis=0)
    xr = cb_ref[...]
    for k in range(CONV_W):
        xr = xr + xe[SUBLANES - 2 + k:SUBLANES - 2 + k + tt, :] * cw_ref[k:k + 1, :]
    xb = xr.astype(BF16)

    ba = rp_ref[0:1, :]
    bi = rp_ref[1:2, :]
    sp = jax.nn.softplus(-rp_ref[2:3, :])
    for n in range(d // bw):
        cols = slice(n * bw, (n + 1) * bw)
        g = jnp.dot(xb[:, cols], w_ref[n], preferred_element_type=F32)
        r = jax.nn.sigmoid(g[:, :bw] + ba[:, cols])
        ig = jax.nn.sigmoid(g[:, bw:] + bi[:, cols])
        log_a = (-RG_C) * r * sp[:, cols]
        a = jnp.exp(log_a)
        bx = jnp.sqrt(1.0 - a * a) * (ig * xr[:, cols])
        a_ref[:, :, cols] = a.reshape(tt // SUBLANES, SUBLANES, bw)
        b_ref[:, :, cols] = bx.reshape(tt // SUBLANES, SUBLANES, bw)

    ng = tt // SUBLANES

    def run(reverse):
        a3, b3 = _scan_tile(a_ref[...], b_ref[...], None, reverse)
        a_ref[...] = a3
        b_ref[...] = b3

        def body(s, carry):
            g = (ng - 1 - s) if reverse else s
            hg = b_ref[g] + a_ref[g] * carry
            b_ref[g] = hg
            edge = hg[0:1, :] if reverse else hg[SUBLANES - 1:SUBLANES, :]
            return jnp.broadcast_to(edge, hg.shape)

        h_ref[...] = lax.fori_loop(0, ng, body, h_ref[...])

    @pl.when(dr == 0)
    def _():
        run(False)

    @pl.when(dr == 1)
    def _():
        run(True)

    o_ref[...] = b_ref[...].reshape(tt, d).astype(BF16)


def _tile_of(dr, i, nt, nctx):
    back = jnp.where(i < nctx, nctx - 1 - i, nt - 1 - (i - nctx))
    return jnp.where(dr == 0, i, back)


def _rglru(p3, conv_w, conv_b, rg_w, rg_p, l, geo, off_rx):
    b, lc, _ = p3.shape
    d = conv_w.shape[-1]
    bw = d // RG_BLOCKS
    tt = ROW_BLOCK
    nt = lc // tt
    nctx = geo["nctx"]
    xb = off_rx // d
    hb = tt // 16
    nh = lc // 16

    def tile(dr, i):
        return _tile_of(dr, i, nt, nctx)

    kern = functools.partial(_rglru_kernel, tt=tt, nt=nt, nctx=nctx, d=d, bw=bw)
    return pl.pallas_call(
        kern,
        grid=(2, b, nt),
        in_specs=[
            pl.BlockSpec((None, tt, d), lambda dr, bi, i: (bi, tile(dr, i), xb)),
            pl.BlockSpec((None, 16, d), lambda dr, bi, i: (bi, jnp.maximum(tile(dr, i) * hb - 1, 0), xb)),
            pl.BlockSpec((None, 16, d), lambda dr, bi, i: (bi, jnp.minimum((tile(dr, i) + 1) * hb, nh - 1), xb)),
            pl.BlockSpec((None, CONV_W, d), lambda dr, bi, i: (l, 0, 0)),
            pl.BlockSpec((None, 1, d), lambda dr, bi, i: (l, 0, 0)),
            pl.BlockSpec((None, None, RG_BLOCKS, bw, 2 * bw), lambda dr, bi, i: (l, dr, 0, 0, 0)),
            pl.BlockSpec((None, None, 3, d), lambda dr, bi, i: (l, dr, 0, 0)),
        ],
        out_specs=pl.BlockSpec((None, None, tt, d), lambda dr, bi, i: (dr, bi, tile(dr, i), 0)),
        out_shape=jax.ShapeDtypeStruct((2, b, lc, d), BF16),
        scratch_shapes=[
            pltpu.VMEM((tt // SUBLANES, SUBLANES, d), F32),
            pltpu.VMEM((tt // SUBLANES, SUBLANES, d), F32),
            pltpu.VMEM((SUBLANES, d), F32),
        ],
        compiler_params=_params("arbitrary", "arbitrary", "arbitrary"),
        name="rglru",
    )(p3, p3, p3, conv_w, conv_b, rg_w, rg_p)


def _merge_kernel(o_ref, hf_ref, hb_ref, ry_ref, gm_ref, gr_ref, wm_ref, wr_ref, z_ref, r_ref):
    @pl.when(pl.program_id(1) == 0)
    def _():
        h = hf_ref[...].astype(F32) + hb_ref[...].astype(F32)
        r_ref[...] = (h * jax.nn.gelu(ry_ref[...].astype(F32))).astype(BF16)

    ym = jnp.dot(o_ref[...], wm_ref[...], preferred_element_type=F32)
    yr = jnp.dot(r_ref[...], wr_ref[...], preferred_element_type=F32)
    z = jax.nn.sigmoid(gm_ref[...].astype(F32)) * ym + jax.nn.sigmoid(gr_ref[...].astype(F32)) * yr
    z_ref[...] = z.astype(BF16)


def _merge(o2, h2d, p, w_o_mla, w_o_rnn, l, off_ry, off_gm, off_gr):
    m, d = o2.shape
    tm = _pick(m, (512, 256))
    tn = _pick(d, (512, 256))
    ryb = off_ry // d
    gmb = off_gm // tn
    grb = off_gr // tn
    return pl.pallas_call(
        _merge_kernel,
        grid=(m // tm, d // tn),
        in_specs=[
            pl.BlockSpec((tm, d), lambda i, j: (i, 0)),
            pl.BlockSpec((None, tm, d), lambda i, j: (0, i, 0)),
            pl.BlockSpec((None, tm, d), lambda i, j: (1, i, 0)),
            pl.BlockSpec((tm, d), lambda i, j: (i, ryb)),
            pl.BlockSpec((tm, tn), lambda i, j: (i, gmb + j)),
            pl.BlockSpec((tm, tn), lambda i, j: (i, grb + j)),
            pl.BlockSpec((None, d, tn), lambda i, j: (l, 0, j)),
            pl.BlockSpec((None, d, tn), lambda i, j: (l, 0, j)),
        ],
        out_specs=pl.BlockSpec((tm, tn), lambda i, j: (i, j)),
        out_shape=jax.ShapeDtypeStruct((m, d), BF16),
        scratch_shapes=[pltpu.VMEM((tm, d), BF16)],
        compiler_params=_params("arbitrary", "arbitrary"),
        name="merge",
    )(o2, h2d, h2d, p, p, p, w_o_mla, w_o_rnn)


def _outproj_kernel(z_ref, w_ref, x_ref, g_ref, o_ref, *, tm, nt, nctx, nb):
    i = pl.program_id(0)
    y = jnp.dot(z_ref[...], w_ref[...], preferred_element_type=F32)
    for s in range(tm // ROW_BLOCK):
        rows = slice(s * ROW_BLOCK, (s + 1) * ROW_BLOCK)
        row = _mod_row(i * (tm // ROW_BLOCK) + s, nt, nctx, nb)
        o_ref[rows, :] = x_ref[rows, :] + g_ref[pl.ds(row, 1), :] * y[rows, :]


def _outproj(z, w_out, x2, mod_t, l, geo):
    m, d = x2.shape
    tm = _pick(m, (1024, 512, 256))
    tn = _pick(d, (512, 256))
    kern = functools.partial(_outproj_kernel, tm=tm, nt=geo["nt"], nctx=geo["nctx"], nb=geo["nb"])
    return pl.pallas_call(
        kern,
        grid=(m // tm, d // tn),
        in_specs=[
            pl.BlockSpec((tm, d), lambda i, j: (i, 0)),
            pl.BlockSpec((None, d, tn), lambda i, j: (l, 0, j)),
            pl.BlockSpec((tm, tn), lambda i, j: (i, j)),
            pl.BlockSpec((None, None, 16, tn), lambda i, j: (l, 2, 0, j)),
        ],
        out_specs=pl.BlockSpec((tm, tn), lambda i, j: (i, j)),
        out_shape=jax.ShapeDtypeStruct((m, d), F32),
        input_output_aliases={2: 0},
        compiler_params=_params("arbitrary", "arbitrary"),
        name="outproj",
    )(z, w_out, x2, mod_t)


def _pairs(per):
    return [(a, c) for a in range(per) for c in range(a + 1, per)]


def _router(h, wr_hi, wr_lo, rb):
    hi = h.astype(BF16)
    lo = (h - hi.astype(F32)).astype(BF16)
    nt_dims = (((1,), (1,)), ((), ()))
    logits = (lax.dot_general(wr_hi, hi, nt_dims, preferred_element_type=F32)
              + lax.dot_general(wr_hi, lo, nt_dims, preferred_element_type=F32)
              + lax.dot_general(wr_lo, hi, nt_dims, preferred_element_type=F32))
    scores = jax.nn.sigmoid(logits)
    biased = scores + rb
    e = scores.shape[0]
    per = e // N_GROUPS
    rows_b = [biased[j:j + 1, :] for j in range(e)]
    rows_s = [scores[j:j + 1, :] for j in range(e)]
    gscore = []
    for g in range(N_GROUPS):
        r = rows_b[g * per:(g + 1) * per]
        best = None
        for a in range(per):
            for c in range(a + 1, per):
                pair = r[a] + r[c]
                best = pair if best is None else jnp.maximum(best, pair)
        gscore.append(best)
    gbest = gscore[0]
    gidx = jnp.zeros_like(gbest, dtype=jnp.int32)
    for g in range(1, N_GROUPS):
        better = gscore[g] > gbest
        gbest = jnp.where(better, gscore[g], gbest)
        gidx = jnp.where(better, g, gidx)
    sel = []
    for g in range(N_GROUPS):
        r = rows_b[g * per:(g + 1) * per]
        for a in range(per):
            rank = jnp.zeros_like(gidx)
            for c in range(per):
                if c == a:
                    continue
                ahead = (r[c] > r[a]) if c > a else (r[c] >= r[a])
                rank = rank + ahead.astype(jnp.int32)
            sel.append(jnp.logical_and(gidx == g, rank < 2))
    den = None
    for j in range(e):
        term = jnp.where(sel[j], rows_s[j], 0.0)
        den = term if den is None else den + term
    cls = jnp.zeros_like(gidx)
    w_lo = jnp.zeros_like(den)
    w_hi = jnp.zeros_like(den)
    pairs = _pairs(per)
    for g in range(N_GROUPS):
        for pi, (a, c) in enumerate(pairs):
            both = jnp.logical_and(sel[g * per + a], sel[g * per + c])
            cls = jnp.where(both, g * len(pairs) + pi, cls)
            w_lo = jnp.where(both, rows_s[g * per + a] / den, w_lo)
            w_hi = jnp.where(both, rows_s[g * per + c] / den, w_hi)
    return cls, w_lo, w_hi


def _ffnprep_kernel(x_ref, g_ref, sh_ref, sc_ref, wh_ref, wl_ref, rb_ref, hs_ref, meta_ref, cnt_ref, run_ref,
                    *, tm, nt, nctx, nb, d):
    i = pl.program_id(0)

    @pl.when(i == 0)
    def _():
        run_ref[...] = jnp.zeros_like(run_ref)

    hs = []
    for s in range(tm // ROW_BLOCK):
        rows = slice(s * ROW_BLOCK, (s + 1) * ROW_BLOCK)
        row = _mod_row(i * (tm // ROW_BLOCK) + s, nt, nctx, nb)
        y = _rms(x_ref[rows, :]) * g_ref[...]
        y = y * (1.0 + sc_ref[pl.ds(row, 1), :]) + sh_ref[pl.ds(row, 1), :]
        hs_ref[rows, 0:d] = y
        hs.append(y)
    h = jnp.concatenate(hs, axis=0) if len(hs) > 1 else hs[0]
    cls, w_lo, w_hi = _router(h, wh_ref[...], wl_ref[...], rb_ref[...])
    extra = jnp.concatenate([w_lo, w_hi, jnp.zeros((LANES - 2, tm), F32)], axis=0)
    hs_ref[:, d:d + LANES] = jnp.transpose(extra)

    nc = run_ref.shape[0]
    onehot = (lax.broadcasted_iota(jnp.int32, (nc, tm), 0) == cls).astype(F32)
    before = lax.broadcasted_iota(jnp.int32, (tm, tm), 0) < lax.broadcasted_iota(jnp.int32, (tm, tm), 1)
    prefix = jnp.dot(onehot.astype(BF16), before.astype(BF16), preferred_element_type=F32)
    run = run_ref[:, 0:1]
    rank = jnp.sum(onehot * (prefix + run), axis=0, keepdims=True).astype(jnp.int32)
    meta_ref[...] = jnp.concatenate([cls, rank, jnp.zeros((SUBLANES - 2, tm), jnp.int32)], axis=0)
    run_ref[...] = run_ref[...] + jnp.sum(onehot, axis=1, keepdims=True)
    cnt_ref[...] = run_ref[...]


def _ffnprep(x2, norm_g, mod_t, wr_hi, wr_lo, rbias, l, geo, ncls):
    m, d = x2.shape
    e = wr_hi.shape[0]
    tm = _pick(m, (512, 256))
    nc = -(-ncls // SUBLANES) * SUBLANES
    kern = functools.partial(_ffnprep_kernel, tm=tm, nt=geo["nt"], nctx=geo["nctx"], nb=geo["nb"], d=d)
    return pl.pallas_call(
        kern,
        grid=(m // tm,),
        in_specs=[
            pl.BlockSpec((tm, d), lambda i: (i, 0)),
            pl.BlockSpec((None, 1, d), lambda i: (l, 0, 0)),
            pl.BlockSpec((None, None, 16, d), lambda i: (l, 3, 0, 0)),
            pl.BlockSpec((None, None, 16, d), lambda i: (l, 4, 0, 0)),
            pl.BlockSpec((e, d), lambda i: (0, 0)),
            pl.BlockSpec((e, d), lambda i: (0, 0)),
            pl.BlockSpec((e, 1), lambda i: (0, 0)),
        ],
        out_specs=[
            pl.BlockSpec((tm, d + LANES), lambda i: (i, 0)),
            pl.BlockSpec((SUBLANES, tm), lambda i: (0, i)),
            pl.BlockSpec((nc, LANES), lambda i: (0, 0)),
        ],
        out_shape=[jax.ShapeDtypeStruct((m, d + LANES), F32), jax.ShapeDtypeStruct((SUBLANES, m), jnp.int32),
                   jax.ShapeDtypeStruct((nc, LANES), F32)],
        scratch_shapes=[pltpu.VMEM((nc, LANES), F32)],
        compiler_params=_params("arbitrary"),
        name="ffnprep",
    )(x2, norm_g, mod_t, mod_t, wr_hi, wr_lo, rbias)


def _row_copy(src_ref, src_row, dst_ref, dst_row, sem):
    return pltpu.make_async_copy(src_ref.at[pl.ds(src_row, 1)], dst_ref.at[pl.ds(dst_row, 1)], sem)


def _dispatch_kernel(pos_ref, hs_ref, xs_in_ref, xs_ref, sem, *, tm):
    del xs_in_ref
    base = pl.program_id(0) * tm

    def start(r, carry):
        _row_copy(hs_ref, r, xs_ref, pos_ref[base + r], sem).start()
        return carry

    def wait(r, carry):
        _row_copy(hs_ref, r, xs_ref, pos_ref[base + r], sem).wait()
        return carry

    lax.fori_loop(0, tm, start, 0)
    lax.fori_loop(0, tm, wait, 0)


def _dispatch(pos, hs, xs0):
    m, w = hs.shape
    tm = _pick(m, (512, 256))
    return pl.pallas_call(
        functools.partial(_dispatch_kernel, tm=tm),
        grid_spec=pltpu.PrefetchScalarGridSpec(
            num_scalar_prefetch=1,
            grid=(m // tm,),
            in_specs=[
                pl.BlockSpec((tm, w), lambda i, pos: (i, 0)),
                pl.BlockSpec(memory_space=pl.ANY),
            ],
            out_specs=pl.BlockSpec(memory_space=pl.ANY),
            scratch_shapes=[pltpu.SemaphoreType.DMA(())],
        ),
        out_shape=jax.ShapeDtypeStruct(xs0.shape, F32),
        input_output_aliases={2: 0},
        compiler_params=_params("arbitrary"),
        name="dispatch",
    )(pos, hs, xs0)


def _expert_kernel(eid_ref, used_ref, xs_ref, wg_ref, wu_ref, wd_ref, *rest, d, k):
    del eid_ref
    o_ref = rest[-1]
    t = pl.program_id(0)

    @pl.when(t < used_ref[0])
    def _():
        x = xs_ref[:, 0:d].astype(BF16)
        gate = jnp.dot(x, wg_ref[...], preferred_element_type=F32)
        up = jnp.dot(x, wu_ref[...], preferred_element_type=F32)
        act = (gate * jax.nn.sigmoid(gate) * up).astype(BF16)
        y = jnp.dot(act, wd_ref[...], preferred_element_type=F32)
        y = xs_ref[:, d + k:d + k + 1] * y
        o_ref[...] = y if k == 0 else rest[0][...] + y

    @pl.when(t >= used_ref[0])
    def _():
        o_ref[...] = jnp.zeros_like(o_ref)


def _experts(eid, used, xs, w_gate, w_up, w_down, prev, l, tm, k):
    p, w = xs.shape
    d = w - LANES
    f = w_gate.shape[-1]
    nt = p // tm
    row_spec = pl.BlockSpec((tm, d), lambda t, eid, used: (t, 0))
    return pl.pallas_call(
        functools.partial(_expert_kernel, d=d, k=k),
        grid_spec=pltpu.PrefetchScalarGridSpec(
            num_scalar_prefetch=2,
            grid=(nt,),
            in_specs=[
                pl.BlockSpec((tm, w), lambda t, eid, used: (t, 0)),
                pl.BlockSpec((None, None, d, f), lambda t, eid, used: (l, eid[k * nt + t], 0, 0)),
                pl.BlockSpec((None, None, d, f), lambda t, eid, used: (l, eid[k * nt + t], 0, 0)),
                pl.BlockSpec((None, None, f, d), lambda t, eid, used: (l, eid[k * nt + t], 0, 0)),
            ] + [row_spec] * len(prev),
            out_specs=row_spec,
        ),
        out_shape=jax.ShapeDtypeStruct((p, d), F32),
        compiler_params=_params("arbitrary"),
        name="experts",
    )(eid, used, xs, w_gate, w_up, w_down, *prev)


def _combine_kernel(pos_ref, x_ref, ys_ref, g_ref, o_ref, y_ref, sem, *, tm, nt, nctx, nb):
    i = pl.program_id(0)
    base = i * tm

    def start(r, carry):
        _row_copy(ys_ref, pos_ref[base + r], y_ref, r, sem).start()
        return carry

    def wait(r, carry):
        _row_copy(ys_ref, pos_ref[base + r], y_ref, r, sem).wait()
        return carry

    lax.fori_loop(0, tm, start, 0)
    lax.fori_loop(0, tm, wait, 0)
    for s in range(tm // ROW_BLOCK):
        rows = slice(s * ROW_BLOCK, (s + 1) * ROW_BLOCK)
        row = _mod_row(i * (tm // ROW_BLOCK) + s, nt, nctx, nb)
        o_ref[rows, :] = x_ref[rows, :] + g_ref[pl.ds(row, 1), :] * y_ref[rows, :]


def _combine(pos, x2, ys, mod_t, l, geo):
    m, d = x2.shape
    tm = _pick(m, (512, 256))
    kern = functools.partial(_combine_kernel, tm=tm, nt=geo["nt"], nctx=geo["nctx"], nb=geo["nb"])
    return pl.pallas_call(
        kern,
        grid_spec=pltpu.PrefetchScalarGridSpec(
            num_scalar_prefetch=1,
            grid=(m // tm,),
            in_specs=[
                pl.BlockSpec((tm, d), lambda i, pos: (i, 0)),
                pl.BlockSpec(memory_space=pl.ANY),
                pl.BlockSpec((None, None, 16, d), lambda i, pos: (l, 5, 0, 0)),
            ],
            out_specs=pl.BlockSpec((tm, d), lambda i, pos: (i, 0)),
            scratch_shapes=[pltpu.VMEM((tm, d), F32), pltpu.SemaphoreType.DMA(())],
        ),
        out_shape=jax.ShapeDtypeStruct((m, d), F32),
        input_output_aliases={1: 0},
        compiler_params=_params("arbitrary"),
        name="combine",
    )(pos, x2, ys, mod_t)


def _moe_ffn(x2, norm_g, mod_t, wr_hi, wr_lo, rbias, w_gate, w_up, w_down, l, geo):
    m, d = x2.shape
    ne = w_gate.shape[1]
    per = ne // N_GROUPS
    pairs = _pairs(per)
    ncls = N_GROUPS * len(pairs)
    tm = ROW_BLOCK
    nt = m // tm + ncls
    hs, meta, cnt = _ffnprep(x2, norm_g, mod_t, wr_hi, wr_lo, rbias, l, geo, ncls)

    count = cnt[:ncls, 0].astype(jnp.int32)
    tiles = (count + tm - 1) // tm
    tile_end = jnp.cumsum(tiles)
    row0 = (tile_end - tiles) * tm
    pos = row0[meta[0]] + meta[1]
    used = tile_end[-1]
    tidx = jnp.minimum(jnp.arange(nt, dtype=jnp.int32), used - 1)
    tile_cls = jnp.searchsorted(tile_end, tidx, side="right").astype(jnp.int32)
    e_lo = jnp.array([g * per + a for g in range(N_GROUPS) for a, _ in pairs], jnp.int32)
    e_hi = jnp.array([g * per + c for g in range(N_GROUPS) for _, c in pairs], jnp.int32)
    eid = jnp.concatenate([e_lo[tile_cls], e_hi[tile_cls]])

    xs = _dispatch(pos, hs, jnp.zeros((nt * tm, d + LANES), F32))
    used = used.reshape(1)
    ys = _experts(eid, used, xs, w_gate, w_up, w_down, (), l, tm, 0)
    ys = _experts(eid, used, xs, w_gate, w_up, w_down, (ys,), l, tm, 1)
    return _combine(pos, x2, ys, mod_t, l, geo)


def _final_kernel(x_ref, g_ref, o_ref):
    o_ref[...] = _rms(x_ref[...]) * g_ref[...]


def _final_norm(x3, g, c_rows, s_rows):
    b, lc, d = x3.shape
    tm = ROW_BLOCK
    off = c_rows // tm
    return pl.pallas_call(
        _final_kernel,
        grid=(b, s_rows // tm),
        in_specs=[
            pl.BlockSpec((None, tm, d), lambda bi, i: (bi, off + i, 0)),
            pl.BlockSpec((1, d), lambda bi, i: (0, 0)),
        ],
        out_specs=pl.BlockSpec((None, tm, d), lambda bi, i: (bi, i, 0)),
        out_shape=jax.ShapeDtypeStruct((b, s_rows, d), F32),
        compiler_params=_params("arbitrary", "arbitrary"),
        name="final_norm",
    )(x3, g.reshape(1, d))


def _rope_tables(c_rows, s_rows):
    rows = s_rows // GRID_W
    row = jnp.broadcast_to(jnp.arange(rows)[:, None], (rows, GRID_W)).reshape(-1).astype(F32)
    col = jnp.broadcast_to(jnp.arange(GRID_W)[None, :], (rows, GRID_W)).reshape(-1).astype(F32)
    half = QK_ROPE // 2
    inv = ROPE_THETA ** (-jnp.arange(0, half, 2, dtype=F32) / half)
    ang_r = row[:, None] * inv
    ang_c = col[:, None] * inv
    ang = jnp.concatenate([ang_r, ang_r, ang_c, ang_c], axis=-1)
    ang = jnp.concatenate([jnp.zeros((c_rows, QK_ROPE), F32), ang], axis=0)
    ang = jnp.concatenate([ang, ang], axis=-1)
    lane = jnp.arange(LANES)
    sign = jnp.where((lane % 32) < 16, -1.0, 1.0).astype(F32)
    return jnp.cos(ang), jnp.sin(ang) * sign


def kernel(x, c, ctx, c_ctx, w_mod, b_mod, norm_mix_g, norm_ffn_g, w_in, kv_norm_g, w_ukv, conv_w, conv_b,
           rg_wa, rg_ba, rg_wi, rg_bi, rg_lambda, w_o_mla, w_o_rnn, w_out, w_router, router_bias,
           w_gate, w_up, w_down, final_norm_g):
    b, s_rows, d = x.shape
    c_rows = ctx.shape[1]
    depth = w_mod.shape[0]
    kv_rank = kv_norm_g.shape[-1]
    lc = c_rows + s_rows
    m = b * lc
    assert c_rows % ROW_BLOCK == 0 and s_rows % ROW_BLOCK == 0 and b < 16
    geo = {"nt": lc // ROW_BLOCK, "nctx": c_rows // ROW_BLOCK, "nb": b}

    nq = N_HEADS * QK_HEAD
    wq = w_in[:, :, :nq].reshape(depth, d, N_HEADS, QK_HEAD)
    off_qr = N_HEADS * QK_NOPE
    off_ckv = off_qr + N_HEADS * QK_ROPE
    off_kr = off_ckv + kv_rank
    off_rx = -(-(off_kr + LANES) // d) * d
    off_ry, off_gm, off_gr = off_rx + d, off_rx + 2 * d, off_rx + 3 * d
    tail = nq + kv_rank + QK_ROPE
    w_in_p = jnp.concatenate([
        wq[..., :QK_NOPE].reshape(depth, d, -1),
        wq[..., QK_NOPE:].reshape(depth, d, -1),
        w_in[:, :, nq:tail],
        jnp.zeros((depth, d, off_rx - off_kr - QK_ROPE), w_in.dtype),
        w_in[:, :, tail:],
    ], axis=-1).astype(BF16)
    wkv = w_ukv.reshape(depth, kv_rank, N_HEADS, QK_NOPE + V_HEAD)
    w_ukv_p = jnp.concatenate([wkv[..., :QK_NOPE].reshape(depth, kv_rank, -1),
                               wkv[..., QK_NOPE:].reshape(depth, kv_rank, -1)], axis=-1).astype(BF16)
    rg_w = jnp.concatenate([rg_wa, rg_wi], axis=-1).astype(BF16)
    rg_p = jnp.stack([rg_ba, rg_bi, rg_lambda], axis=2)
    w_o_mla_b, w_o_rnn_b, w_out_b = w_o_mla.astype(BF16), w_o_rnn.astype(BF16), w_out.astype(BF16)
    w_gate_b, w_up_b, w_down_b = w_gate.astype(BF16), w_up.astype(BF16), w_down.astype(BF16)
    wr_t = w_router.T
    wr_hi = wr_t.astype(BF16)
    wr_lo = (wr_t - wr_hi.astype(F32)).astype(BF16)
    rbias = router_bias.reshape(-1, 1).astype(F32)
    norm_mix = norm_mix_g.reshape(depth, 1, d)
    norm_ffn = norm_ffn_g.reshape(depth, 1, d)
    kv_g = kv_norm_g.reshape(depth, 1, kv_rank)
    conv_b3 = conv_b.reshape(depth, 1, d)
    cos_t, sin_t = _rope_tables(c_rows, s_rows)

    cc = jnp.concatenate([c, c_ctx[None, :], jnp.zeros((16 - b - 1, d), F32)], axis=0)
    mod_t = _mod_table(cc, w_mod, b_mod).reshape(depth, 16, N_MOD, d).transpose(0, 2, 1, 3)

    x2 = jnp.concatenate([ctx, x], axis=1).reshape(m, d)
    for l in range(depth):
        p = _inproj(x2, norm_mix, mod_t, w_in_p, l, geo)
        p3 = p.reshape(b, lc, -1)
        kv = _kvup(p, kv_g, w_ukv_p, l, off_ckv)
        o = _attention(p3, kv.reshape(b, lc, -1), cos_t, sin_t, geo, off_qr, off_kr)
        hd = _rglru(p3, conv_w, conv_b3, rg_w, rg_p, l, geo, off_rx)
        z = _merge(o.reshape(m, -1), hd.reshape(2, m, d), p, w_o_mla_b, w_o_rnn_b, l, off_ry, off_gm, off_gr)
        x2 = _outproj(z, w_out_b, x2, mod_t, l, geo)
        x2 = _moe_ffn(x2, norm_ffn, mod_t, wr_hi, wr_lo, rbias, w_gate_b, w_up_b, w_down_b, l, geo)
    return _final_norm(x2.reshape(b, lc, d), final_norm_g, c_rows, s_rows)
```

```python
import functools

import jax
import jax.numpy as jnp
from jax import lax
from jax.experimental import pallas as pl
from jax.experimental.pallas import tpu as pltpu

N_HEADS = 16
QK_NOPE = 128
QK_ROPE = 64
QK_HEAD = QK_NOPE + QK_ROPE
V_HEAD = 128
GRID_W = 64
ROPE_THETA = 10000.0
RG_BLOCKS = 8
CONV_W = 4
RG_C = 8.0
N_GROUPS = 4
EPS = 1e-6
N_MOD = 6

LANES = 128
SUBLANES = 8
ROW_BLOCK = 256
KEY_CHUNK = 512
DMA_UNROLL = 8
LOG2E = 1.4426950408889634
VMEM_LIMIT = 56 * 1024 * 1024

F32 = jnp.float32
BF16 = jnp.bfloat16


def _pick(n, candidates):
    for c in candidates:
        if n % c == 0:
            return c
    raise ValueError(f"no tile of {candidates} divides {n}")


def _params(*sem):
    return pltpu.CompilerParams(dimension_semantics=sem, vmem_limit_bytes=VMEM_LIMIT)


def _mod_row(blk, nt, nctx, nb):
    return jnp.where(blk % nt >= nt - nctx, nb, blk // nt)


def _rms(x):
    return x * lax.rsqrt(jnp.mean(x * x, axis=-1, keepdims=True) + EPS)


def _mod_kernel(c_ref, w_ref, b_ref, o_ref):
    c = c_ref[...]
    s = (c * jax.nn.sigmoid(c)).astype(BF16)
    o_ref[...] = jnp.dot(s, w_ref[...].astype(BF16), preferred_element_type=F32) + b_ref[...]


def _mod_table(cc, w_mod, b_mod):
    depth, d, n = w_mod.shape
    tn = _pick(n, (1024, 512, 256, 128))
    return pl.pallas_call(
        _mod_kernel,
        grid=(depth, n // tn),
        in_specs=[
            pl.BlockSpec((16, d), lambda l, j: (0, 0)),
            pl.BlockSpec((None, d, tn), lambda l, j: (l, 0, j)),
            pl.BlockSpec((None, 1, tn), lambda l, j: (l, 0, j)),
        ],
        out_specs=pl.BlockSpec((None, 16, tn), lambda l, j: (l, 0, j)),
        out_shape=jax.ShapeDtypeStruct((depth, 16, n), F32),
        compiler_params=_params("arbitrary", "arbitrary"),
        name="mod_table",
    )(cc, w_mod, b_mod.reshape(depth, 1, n))


def _inproj_kernel(x_ref, g_ref, sh_ref, sc_ref, w_ref, o_ref, h_ref, *, tm, nt, nctx, nb):
    i = pl.program_id(0)

    @pl.when(pl.program_id(1) == 0)
    def _():
        for s in range(tm // ROW_BLOCK):
            rows = slice(s * ROW_BLOCK, (s + 1) * ROW_BLOCK)
            row = _mod_row(i * (tm // ROW_BLOCK) + s, nt, nctx, nb)
            y = _rms(x_ref[rows, :]) * g_ref[...]
            y = y * (1.0 + sc_ref[pl.ds(row, 1), :]) + sh_ref[pl.ds(row, 1), :]
            h_ref[rows, :] = y.astype(BF16)

    o_ref[...] = jnp.dot(h_ref[...], w_ref[...], preferred_element_type=F32).astype(BF16)


def _inproj(x2, norm_g, mod_t, w_in_p, l, geo):
    m, d = x2.shape
    n = w_in_p.shape[-1]
    tm = _pick(m, (1024, 512, 256))
    tn = _pick(n, (1024, 512, 256, 128))
    kern = functools.partial(_inproj_kernel, tm=tm, nt=geo["nt"], nctx=geo["nctx"], nb=geo["nb"])
    return pl.pallas_call(
        kern,
        grid=(m // tm, n // tn),
        in_specs=[
            pl.BlockSpec((tm, d), lambda i, j: (i, 0)),
            pl.BlockSpec((None, 1, d), lambda i, j: (l, 0, 0)),
            pl.BlockSpec((None, None, 16, d), lambda i, j: (l, 0, 0, 0)),
            pl.BlockSpec((None, None, 16, d), lambda i, j: (l, 1, 0, 0)),
            pl.BlockSpec((None, d, tn), lambda i, j: (l, 0, j)),
        ],
        out_specs=pl.BlockSpec((tm, tn), lambda i, j: (i, j)),
        out_shape=jax.ShapeDtypeStruct((m, n), BF16),
        scratch_shapes=[pltpu.VMEM((tm, d), BF16)],
        compiler_params=_params("arbitrary", "arbitrary"),
        name="inproj",
    )(x2, norm_g, mod_t, mod_t, w_in_p)


def _kvup_kernel(c_ref, g_ref, w_ref, o_ref, h_ref):
    @pl.when(pl.program_id(1) == 0)
    def _():
        h_ref[...] = (_rms(c_ref[...].astype(F32)) * g_ref[...]).astype(BF16)

    o_ref[...] = jnp.dot(h_ref[...], w_ref[...], preferred_element_type=F32).astype(BF16)


def _kvup(p, kv_g, w_ukv_p, l, off_ckv):
    m = p.shape[0]
    r, n = w_ukv_p.shape[1:]
    tm = _pick(m, (1024, 512, 256))
    tn = _pick(n, (2048, 1024, 512, 256))
    cb = off_ckv // r
    return pl.pallas_call(
        _kvup_kernel,
        grid=(m // tm, n // tn),
        in_specs=[
            pl.BlockSpec((tm, r), lambda i, j: (i, cb)),
            pl.BlockSpec((None, 1, r), lambda i, j: (l, 0, 0)),
            pl.BlockSpec((None, r, tn), lambda i, j: (l, 0, j)),
        ],
        out_specs=pl.BlockSpec((tm, tn), lambda i, j: (i, j)),
        out_shape=jax.ShapeDtypeStruct((m, n), BF16),
        scratch_shapes=[pltpu.VMEM((tm, r), BF16)],
        compiler_params=_params("arbitrary", "arbitrary"),
        name="kvup",
    )(p, kv_g, w_ukv_p)


def _rope(x, cos, sin_signed):
    lane = lax.broadcasted_iota(jnp.int32, x.shape, 1)
    first = (lane % 32) < 16
    rot = jnp.where(first, pltpu.roll(x, LANES - 16, 1), pltpu.roll(x, 16, 1))
    return x * cos + rot * sin_signed


def _key_chunks(n):
    return [(k0, min(KEY_CHUNK, n - k0)) for k0 in range(0, n, KEY_CHUNK)]


def _lane_partial_sum(p):
    out = p[:, 0:LANES]
    for c in range(1, p.shape[1] // LANES):
        out = out + p[:, c * LANES:(c + 1) * LANES]
    return out


def _attn_kernel(qn_ref, qr_ref, kn_ref, kr_ref, v_ref, cos_ref, sin_ref, o_ref, kcat_ref, *, tq, lc):
    qi = pl.program_id(2)

    @pl.when(qi == 0)
    def _():
        kr = _rope(kr_ref[...].astype(F32), cos_ref[...], sin_ref[...])
        for j in range(2):
            kcat_ref[j, :, 0:QK_NOPE] = kn_ref[:, j * QK_NOPE:(j + 1) * QK_NOPE]
            krj = kr if j == 0 else pltpu.roll(kr, QK_ROPE, 1)
            kcat_ref[j, :, QK_NOPE:QK_NOPE + LANES] = krj.astype(BF16)

    scale = QK_HEAD ** -0.5 * LOG2E
    r0 = pl.multiple_of(qi * tq, tq)
    qr = _rope(qr_ref[...].astype(F32), cos_ref[pl.ds(r0, tq), :], sin_ref[pl.ds(r0, tq), :])
    qr = (qr * scale).astype(BF16)
    qn = (qn_ref[...].astype(F32) * scale).astype(BF16)
    nt_dims = (((1,), (1,)), ((), ()))

    for j in range(2):
        q = jnp.concatenate([qn[:, j * QK_NOPE:(j + 1) * QK_NOPE], qr], axis=1)
        m = part = acc = None
        for k0, kn in _key_chunks(lc):
            s = lax.dot_general(q, kcat_ref[j, k0:k0 + kn, :], nt_dims, preferred_element_type=F32)
            smax = jnp.max(s, axis=-1, keepdims=True)
            m_new = smax if m is None else jnp.maximum(m, smax)
            p = jnp.exp2(s - m_new)
            pv = jnp.dot(p.astype(BF16), v_ref[k0:k0 + kn, j * V_HEAD:(j + 1) * V_HEAD],
                         preferred_element_type=F32)
            if m is None:
                part, acc = _lane_partial_sum(p), pv
            else:
                alpha = jnp.exp2(m - m_new)
                part = alpha * part + _lane_partial_sum(p)
                acc = alpha * acc + pv
            m = m_new
        den = jnp.sum(part, axis=-1, keepdims=True)
        o_ref[:, j * V_HEAD:(j + 1) * V_HEAD] = (acc / den).astype(BF16)


def _attn_ctx_kernel(qn_ref, qr_ref, kn_ref, kr_ref, v_ref, o_in_ref, o_ref):
    del o_in_ref
    scale = QK_HEAD ** -0.5
    qr = (qr_ref[...].astype(F32) * scale).astype(BF16)
    qn = (qn_ref[...].astype(F32) * scale).astype(BF16)
    kr = kr_ref[...].astype(F32)
    for j in range(2):
        krj = kr if j == 0 else pltpu.roll(kr, QK_ROPE, 1)
        k = jnp.concatenate([kn_ref[:, j * QK_NOPE:(j + 1) * QK_NOPE], krj.astype(BF16)], axis=1)
        q = jnp.concatenate([qn[:, j * QK_NOPE:(j + 1) * QK_NOPE], qr], axis=1)
        s = lax.dot_general(q, k, (((1,), (1,)), ((), ())), preferred_element_type=F32)
        p = jnp.exp(s - jnp.max(s, axis=-1, keepdims=True))
        den = jnp.sum(p, axis=-1, keepdims=True)
        o = jnp.dot(p.astype(BF16), v_ref[:, j * V_HEAD:(j + 1) * V_HEAD], preferred_element_type=F32)
        o_ref[:, j * V_HEAD:(j + 1) * V_HEAD] = (o / den).astype(BF16)


def _attention(p3, kv3, cos_t, sin_t, s_rows, off_qr, off_kr):
    b, lc, _ = p3.shape
    c_rows = lc - s_rows
    hp = N_HEADS // 2
    tq = _pick(s_rows, (512, 256))
    qrb = off_qr // LANES
    krb = off_kr // LANES
    o = pl.pallas_call(
        functools.partial(_attn_kernel, tq=tq, lc=lc),
        grid=(b, hp, s_rows // tq),
        in_specs=[
            pl.BlockSpec((None, tq, 2 * QK_NOPE), lambda bi, h, q: (bi, q, h)),
            pl.BlockSpec((None, tq, LANES), lambda bi, h, q: (bi, q, qrb + h)),
            pl.BlockSpec((None, lc, 2 * QK_NOPE), lambda bi, h, q: (bi, 0, h)),
            pl.BlockSpec((None, lc, LANES), lambda bi, h, q: (bi, 0, krb)),
            pl.BlockSpec((None, lc, 2 * V_HEAD), lambda bi, h, q: (bi, 0, hp + h)),
            pl.BlockSpec((lc, LANES), lambda bi, h, q: (0, 0)),
            pl.BlockSpec((lc, LANES), lambda bi, h, q: (0, 0)),
        ],
        out_specs=pl.BlockSpec((None, tq, 2 * V_HEAD), lambda bi, h, q: (bi, q, h)),
        out_shape=jax.ShapeDtypeStruct((b, lc, N_HEADS * V_HEAD), BF16),
        scratch_shapes=[pltpu.VMEM((2, lc, QK_NOPE + LANES), BF16)],
        compiler_params=_params("arbitrary", "arbitrary", "arbitrary"),
        name="attention",
    )(p3, p3, kv3, p3, kv3, cos_t, sin_t)
    cb = s_rows // c_rows
    return pl.pallas_call(
        _attn_ctx_kernel,
        grid=(b, hp),
        in_specs=[
            pl.BlockSpec((None, c_rows, 2 * QK_NOPE), lambda bi, h: (bi, cb, h)),
            pl.BlockSpec((None, c_rows, LANES), lambda bi, h: (bi, cb, qrb + h)),
            pl.BlockSpec((None, c_rows, 2 * QK_NOPE), lambda bi, h: (bi, cb, h)),
            pl.BlockSpec((None, c_rows, LANES), lambda bi, h: (bi, cb, krb)),
            pl.BlockSpec((None, c_rows, 2 * V_HEAD), lambda bi, h: (bi, cb, hp + h)),
            pl.BlockSpec(memory_space=pl.ANY),
        ],
        out_specs=pl.BlockSpec((None, c_rows, 2 * V_HEAD), lambda bi, h: (bi, cb, h)),
        out_shape=jax.ShapeDtypeStruct(o.shape, BF16),
        input_output_aliases={5: 0},
        compiler_params=_params("arbitrary", "arbitrary"),
        name="attention_ctx",
    )(p3, p3, kv3, p3, kv3, o)


def _sigmoid(x):
    return 0.5 * jnp.tanh(0.5 * x) + 0.5


def _scan_groups(a3, b3, reverse):
    sub = lax.broadcasted_iota(jnp.int32, a3.shape, 1)
    for k in (1, 2, 4):
        if reverse:
            keep = sub < SUBLANES - k
            shift = SUBLANES - k
        else:
            keep = sub >= k
            shift = k
        a_s = jnp.where(keep, pltpu.roll(a3, shift, 1), 1.0)
        b_s = jnp.where(keep, pltpu.roll(b3, shift, 1), 0.0)
        b3 = b3 + a3 * b_s
        a3 = a3 * a_s
    return a3, b3


def _rglru_kernel(x_ref, xp_ref, xn_ref, cw_ref, cb_ref, w_ref, rp_ref, o_ref, xe_ref, a_ref, b_ref, h_ref,
                  *, tt, nt, nctx, d, bw):
    dr = pl.program_id(0)
    i = pl.program_id(2)
    t = _tile_of(dr, i, nt, nctx)
    nlat = nt - nctx
    first = jnp.logical_or(t == 0, t == nlat)
    last = jnp.logical_or(t == nlat - 1, t == nt - 1)

    @pl.when(i == 0)
    def _():
        h_ref[...] = jnp.zeros_like(h_ref)

    xe_ref[0:SUBLANES, :] = jnp.where(first, 0.0, xp_ref[...].astype(F32)[SUBLANES:, :])
    xe_ref[SUBLANES:SUBLANES + tt, :] = x_ref[...].astype(F32)
    xe_ref[SUBLANES + tt:, :] = jnp.where(last, 0.0, xn_ref[...].astype(F32)[:SUBLANES, :])
    xr = cb_ref[...]
    for k in range(CONV_W):
        xr = xr + xe_ref[SUBLANES - 2 + k:SUBLANES - 2 + k + tt, :] * cw_ref[k:k + 1, :]
    xb = xr.astype(BF16)

    ba = rp_ref[0:1, :]
    bi = rp_ref[1:2, :]
    sp = jax.nn.softplus(-rp_ref[2:3, :])
    for n in range(d // bw):
        cols = slice(n * bw, (n + 1) * bw)
        g = jnp.dot(xb[:, cols], w_ref[n], preferred_element_type=F32)
        r = _sigmoid(g[:, :bw] + ba[:, cols])
        ig = _sigmoid(g[:, bw:] + bi[:, cols])
        log_a = (-RG_C) * r * sp[:, cols]
        a = jnp.exp(log_a)
        bx = jnp.sqrt(1.0 - a * a) * (ig * xr[:, cols])
        a_ref[:, :, cols] = a.reshape(tt // SUBLANES, SUBLANES, bw)
        b_ref[:, :, cols] = bx.reshape(tt // SUBLANES, SUBLANES, bw)

    ng = tt // SUBLANES

    def run(reverse):
        a3, b3 = _scan_groups(a_ref[...], b_ref[...], reverse)
        a_ref[...] = a3
        b_ref[...] = b3

        def body(s, carry):
            g = (ng - 1 - s) if reverse else s
            hg = b_ref[g] + a_ref[g] * carry
            b_ref[g] = hg
            edge = hg[0:1, :] if reverse else hg[SUBLANES - 1:SUBLANES, :]
            return jnp.broadcast_to(edge, hg.shape)

        h_ref[...] = lax.fori_loop(0, ng, body, h_ref[...])

    @pl.when(dr == 0)
    def _():
        run(False)

    @pl.when(dr == 1)
    def _():
        run(True)

    o_ref[...] = b_ref[...].reshape(tt, d).astype(BF16)


def _tile_of(dr, i, nt, nctx):
    nlat = nt - nctx
    fwd = jnp.where(i < nctx, nlat + i, i - nctx)
    back = jnp.where(i < nctx, nt - 1 - i, nlat - 1 - (i - nctx))
    return jnp.where(dr == 0, fwd, back)


def _rglru(p3, conv_w, conv_b, rg_w, rg_p, l, geo, off_rx):
    b, lc, _ = p3.shape
    d = conv_w.shape[-1]
    bw = d // RG_BLOCKS
    tt = ROW_BLOCK
    nt = lc // tt
    nctx = geo["nctx"]
    xb = off_rx // d
    hb = tt // 16
    nh = lc // 16

    def tile(dr, i):
        return _tile_of(dr, i, nt, nctx)

    kern = functools.partial(_rglru_kernel, tt=tt, nt=nt, nctx=nctx, d=d, bw=bw)
    return pl.pallas_call(
        kern,
        grid=(2, b, nt),
        in_specs=[
            pl.BlockSpec((None, tt, d), lambda dr, bi, i: (bi, tile(dr, i), xb)),
            pl.BlockSpec((None, 16, d), lambda dr, bi, i: (bi, jnp.maximum(tile(dr, i) * hb - 1, 0), xb)),
            pl.BlockSpec((None, 16, d), lambda dr, bi, i: (bi, jnp.minimum((tile(dr, i) + 1) * hb, nh - 1), xb)),
            pl.BlockSpec((None, CONV_W, d), lambda dr, bi, i: (l, 0, 0)),
            pl.BlockSpec((None, 1, d), lambda dr, bi, i: (l, 0, 0)),
            pl.BlockSpec((None, None, RG_BLOCKS, bw, 2 * bw), lambda dr, bi, i: (l, dr, 0, 0, 0)),
            pl.BlockSpec((None, None, 3, d), lambda dr, bi, i: (l, dr, 0, 0)),
        ],
        out_specs=pl.BlockSpec((None, None, tt, d), lambda dr, bi, i: (dr, bi, tile(dr, i), 0)),
        out_shape=jax.ShapeDtypeStruct((2, b, lc, d), BF16),
        scratch_shapes=[
            pltpu.VMEM((tt + 2 * SUBLANES, d), F32),
            pltpu.VMEM((tt // SUBLANES, SUBLANES, d), F32),
            pltpu.VMEM((tt // SUBLANES, SUBLANES, d), F32),
            pltpu.VMEM((SUBLANES, d), F32),
        ],
        compiler_params=_params("arbitrary", "arbitrary", "arbitrary"),
        name="rglru",
    )(p3, p3, p3, conv_w, conv_b, rg_w, rg_p)


def _merge_kernel(o_ref, hf_ref, hb_ref, ry_ref, gm_ref, gr_ref, wm_ref, wr_ref, z_ref, r_ref):
    @pl.when(pl.program_id(1) == 0)
    def _():
        h = hf_ref[...].astype(F32) + hb_ref[...].astype(F32)
        r_ref[...] = (h * jax.nn.gelu(ry_ref[...].astype(F32))).astype(BF16)

    ym = jnp.dot(o_ref[...], wm_ref[...], preferred_element_type=F32)
    yr = jnp.dot(r_ref[...], wr_ref[...], preferred_element_type=F32)
    z = jax.nn.sigmoid(gm_ref[...].astype(F32)) * ym + jax.nn.sigmoid(gr_ref[...].astype(F32)) * yr
    z_ref[...] = z.astype(BF16)


def _merge(o2, h2d, p, w_o_mla, w_o_rnn, l, off_ry, off_gm, off_gr):
    m, d = o2.shape
    tm = ROW_BLOCK
    tn = d
    ryb = off_ry // d
    gmb = off_gm // tn
    grb = off_gr // tn
    return pl.pallas_call(
        _merge_kernel,
        grid=(m // tm, d // tn),
        in_specs=[
            pl.BlockSpec((tm, d), lambda i, j: (i, 0)),
            pl.BlockSpec((None, tm, d), lambda i, j: (0, i, 0)),
            pl.BlockSpec((None, tm, d), lambda i, j: (1, i, 0)),
            pl.BlockSpec((tm, d), lambda i, j: (i, ryb)),
            pl.BlockSpec((tm, tn), lambda i, j: (i, gmb + j)),
            pl.BlockSpec((tm, tn), lambda i, j: (i, grb + j)),
            pl.BlockSpec((None, d, tn), lambda i, j: (l, 0, j)),
            pl.BlockSpec((None, d, tn), lambda i, j: (l, 0, j)),
        ],
        out_specs=pl.BlockSpec((tm, tn), lambda i, j: (i, j)),
        out_shape=jax.ShapeDtypeStruct((m, d), BF16),
        scratch_shapes=[pltpu.VMEM((tm, d), BF16)],
        compiler_params=_params("arbitrary", "arbitrary"),
        name="merge",
    )(o2, h2d, h2d, p, p, p, w_o_mla, w_o_rnn)


def _outproj_kernel(z_ref, w_ref, x_ref, g_ref, o_ref, *, tm, nt, nctx, nb):
    i = pl.program_id(0)
    y = jnp.dot(z_ref[...], w_ref[...], preferred_element_type=F32)
    for s in range(tm // ROW_BLOCK):
        rows = slice(s * ROW_BLOCK, (s + 1) * ROW_BLOCK)
        row = _mod_row(i * (tm // ROW_BLOCK) + s, nt, nctx, nb)
        o_ref[rows, :] = x_ref[rows, :] + g_ref[pl.ds(row, 1), :] * y[rows, :]


def _outproj(z, w_out, x2, mod_t, l, geo):
    m, d = x2.shape
    tm = _pick(m, (1024, 512, 256))
    tn = _pick(d, (512, 256))
    kern = functools.partial(_outproj_kernel, tm=tm, nt=geo["nt"], nctx=geo["nctx"], nb=geo["nb"])
    return pl.pallas_call(
        kern,
        grid=(m // tm, d // tn),
        in_specs=[
            pl.BlockSpec((tm, d), lambda i, j: (i, 0)),
            pl.BlockSpec((None, d, tn), lambda i, j: (l, 0, j)),
            pl.BlockSpec((tm, tn), lambda i, j: (i, j)),
            pl.BlockSpec((None, None, 16, tn), lambda i, j: (l, 2, 0, j)),
        ],
        out_specs=pl.BlockSpec((tm, tn), lambda i, j: (i, j)),
        out_shape=jax.ShapeDtypeStruct((m, d), F32),
        input_output_aliases={2: 0},
        compiler_params=_params("arbitrary", "arbitrary"),
        name="outproj",
    )(z, w_out, x2, mod_t)


def _pairs(per):
    return [(a, c) for a in range(per) for c in range(a + 1, per)]


def _router(h, wr_hi, wr_lo, rb):
    hi = h.astype(BF16)
    lo = (h - hi.astype(F32)).astype(BF16)
    nt_dims = (((1,), (1,)), ((), ()))
    logits = (lax.dot_general(wr_hi, hi, nt_dims, preferred_element_type=F32)
              + lax.dot_general(wr_hi, lo, nt_dims, preferred_element_type=F32)
              + lax.dot_general(wr_lo, hi, nt_dims, preferred_element_type=F32))
    scores = jax.nn.sigmoid(logits)
    biased = scores + rb
    e = scores.shape[0]
    per = e // N_GROUPS
    rows_b = [biased[j:j + 1, :] for j in range(e)]
    rows_s = [scores[j:j + 1, :] for j in range(e)]
    gscore = []
    for g in range(N_GROUPS):
        r = rows_b[g * per:(g + 1) * per]
        best = None
        for a in range(per):
            for c in range(a + 1, per):
                pair = r[a] + r[c]
                best = pair if best is None else jnp.maximum(best, pair)
        gscore.append(best)
    gbest = gscore[0]
    gidx = jnp.zeros_like(gbest, dtype=jnp.int32)
    for g in range(1, N_GROUPS):
        better = gscore[g] > gbest
        gbest = jnp.where(better, gscore[g], gbest)
        gidx = jnp.where(better, g, gidx)
    sel = []
    for g in range(N_GROUPS):
        r = rows_b[g * per:(g + 1) * per]
        for a in range(per):
            rank = jnp.zeros_like(gidx)
            for c in range(per):
                if c == a:
                    continue
                ahead = (r[c] > r[a]) if c > a else (r[c] >= r[a])
                rank = rank + ahead.astype(jnp.int32)
            sel.append(jnp.logical_and(gidx == g, rank < 2))
    den = None
    for j in range(e):
        term = jnp.where(sel[j], rows_s[j], 0.0)
        den = term if den is None else den + term
    cls = jnp.zeros_like(gidx)
    w_lo = jnp.zeros_like(den)
    w_hi = jnp.zeros_like(den)
    pairs = _pairs(per)
    for g in range(N_GROUPS):
        for pi, (a, c) in enumerate(pairs):
            both = jnp.logical_and(sel[g * per + a], sel[g * per + c])
            cls = jnp.where(both, g * len(pairs) + pi, cls)
            w_lo = jnp.where(both, rows_s[g * per + a] / den, w_lo)
            w_hi = jnp.where(both, rows_s[g * per + c] / den, w_hi)
    return cls, w_lo, w_hi


def _ffnprep_kernel(x_ref, g_ref, sh_ref, sc_ref, wh_ref, wl_ref, rb_ref, hs_ref, meta_ref, cnt_ref, run_ref,
                    *, tm, nt, nctx, nb, d):
    i = pl.program_id(0)

    @pl.when(i == 0)
    def _():
        run_ref[...] = jnp.zeros_like(run_ref)

    hs = []
    for s in range(tm // ROW_BLOCK):
        rows = slice(s * ROW_BLOCK, (s + 1) * ROW_BLOCK)
        row = _mod_row(i * (tm // ROW_BLOCK) + s, nt, nctx, nb)
        y = _rms(x_ref[rows, :]) * g_ref[...]
        y = y * (1.0 + sc_ref[pl.ds(row, 1), :]) + sh_ref[pl.ds(row, 1), :]
        hs_ref[rows, 0:d] = y
        hs.append(y)
    h = jnp.concatenate(hs, axis=0) if len(hs) > 1 else hs[0]
    cls, w_lo, w_hi = _router(h, wh_ref[...], wl_ref[...], rb_ref[...])
    extra = jnp.concatenate([w_lo, w_hi, jnp.zeros((LANES - 2, tm), F32)], axis=0)
    hs_ref[:, d:d + LANES] = jnp.transpose(extra)

    nc = run_ref.shape[0]
    onehot = (lax.broadcasted_iota(jnp.int32, (nc, tm), 0) == cls).astype(F32)
    before = lax.broadcasted_iota(jnp.int32, (tm, tm), 0) < lax.broadcasted_iota(jnp.int32, (tm, tm), 1)
    prefix = jnp.dot(onehot.astype(BF16), before.astype(BF16), preferred_element_type=F32)
    run = run_ref[:, 0:1]
    rank = jnp.sum(onehot * (prefix + run), axis=0, keepdims=True).astype(jnp.int32)
    meta_ref[...] = jnp.concatenate([cls, rank, jnp.zeros((SUBLANES - 2, tm), jnp.int32)], axis=0)
    run_ref[...] = run_ref[...] + jnp.sum(onehot, axis=1, keepdims=True)
    cnt_ref[...] = run_ref[...]


def _ffnprep(x2, norm_g, mod_t, wr_hi, wr_lo, rbias, l, geo, ncls):
    m, d = x2.shape
    e = wr_hi.shape[0]
    tm = _pick(m, (512, 256))
    nc = -(-ncls // SUBLANES) * SUBLANES
    kern = functools.partial(_ffnprep_kernel, tm=tm, nt=geo["nt"], nctx=geo["nctx"], nb=geo["nb"], d=d)
    return pl.pallas_call(
        kern,
        grid=(m // tm,),
        in_specs=[
            pl.BlockSpec((tm, d), lambda i: (i, 0)),
            pl.BlockSpec((None, 1, d), lambda i: (l, 0, 0)),
            pl.BlockSpec((None, None, 16, d), lambda i: (l, 3, 0, 0)),
            pl.BlockSpec((None, None, 16, d), lambda i: (l, 4, 0, 0)),
            pl.BlockSpec((e, d), lambda i: (0, 0)),
            pl.BlockSpec((e, d), lambda i: (0, 0)),
            pl.BlockSpec((e, 1), lambda i: (0, 0)),
        ],
        out_specs=[
            pl.BlockSpec((tm, d + LANES), lambda i: (i, 0)),
            pl.BlockSpec((SUBLANES, tm), lambda i: (0, i)),
            pl.BlockSpec((nc, LANES), lambda i: (0, 0)),
        ],
        out_shape=[jax.ShapeDtypeStruct((m, d + LANES), F32), jax.ShapeDtypeStruct((SUBLANES, m), jnp.int32),
                   jax.ShapeDtypeStruct((nc, LANES), F32)],
        scratch_shapes=[pltpu.VMEM((nc, LANES), F32)],
        compiler_params=_params("arbitrary"),
        name="ffnprep",
    )(x2, norm_g, mod_t, mod_t, wr_hi, wr_lo, rbias)


def _row_copy(src_ref, src_row, dst_ref, dst_row, sem):
    return pltpu.make_async_copy(src_ref.at[pl.ds(src_row, 1)], dst_ref.at[pl.ds(dst_row, 1)], sem)


def _dispatch_kernel(pos_ref, hs_ref, xs_in_ref, xs_ref, sem, *, tm):
    del xs_in_ref
    base = pl.program_id(0) * tm

    def start(r, carry):
        _row_copy(hs_ref, r, xs_ref, pos_ref[base + r], sem).start()
        return carry

    def wait(r, carry):
        _row_copy(hs_ref, r, xs_ref, pos_ref[base + r], sem).wait()
        return carry

    lax.fori_loop(0, tm, start, 0, unroll=DMA_UNROLL)
    lax.fori_loop(0, tm, wait, 0, unroll=DMA_UNROLL)


def _dispatch(pos, hs, xs0):
    m, w = hs.shape
    tm = _pick(m, (512, 256))
    return pl.pallas_call(
        functools.partial(_dispatch_kernel, tm=tm),
        grid_spec=pltpu.PrefetchScalarGridSpec(
            num_scalar_prefetch=1,
            grid=(m // tm,),
            in_specs=[
                pl.BlockSpec((tm, w), lambda i, pos: (i, 0)),
                pl.BlockSpec(memory_space=pl.ANY),
            ],
            out_specs=pl.BlockSpec(memory_space=pl.ANY),
            scratch_shapes=[pltpu.SemaphoreType.DMA(())],
        ),
        out_shape=jax.ShapeDtypeStruct(xs0.shape, F32),
        input_output_aliases={2: 0},
        compiler_params=_params("arbitrary"),
        name="dispatch",
    )(pos, hs, xs0)


def _expert_kernel(eid_ref, used_ref, xs_ref, wg_ref, wu_ref, wd_ref, *rest, d, k):
    del eid_ref
    o_ref = rest[-1]
    t = pl.program_id(0)

    @pl.when(t < used_ref[0])
    def _():
        x = xs_ref[:, 0:d].astype(BF16)
        gate = jnp.dot(x, wg_ref[...], preferred_element_type=F32)
        up = jnp.dot(x, wu_ref[...], preferred_element_type=F32)
        act = (gate * jax.nn.sigmoid(gate) * up).astype(BF16)
        y = jnp.dot(act, wd_ref[...], preferred_element_type=F32)
        y = xs_ref[:, d + k:d + k + 1] * y
        o_ref[...] = y if k == 0 else rest[0][...] + y

    @pl.when(t >= used_ref[0])
    def _():
        o_ref[...] = jnp.zeros_like(o_ref)


def _experts(eid, used, xs, w_gate, w_up, w_down, prev, l, tm, k):
    p, w = xs.shape
    d = w - LANES
    f = w_gate.shape[-1]
    nt = p // tm
    row_spec = pl.BlockSpec((tm, d), lambda t, eid, used: (t, 0))
    return pl.pallas_call(
        functools.partial(_expert_kernel, d=d, k=k),
        grid_spec=pltpu.PrefetchScalarGridSpec(
            num_scalar_prefetch=2,
            grid=(nt,),
            in_specs=[
                pl.BlockSpec((tm, w), lambda t, eid, used: (t, 0)),
                pl.BlockSpec((None, None, d, f), lambda t, eid, used: (l, eid[k * nt + t], 0, 0)),
                pl.BlockSpec((None, None, d, f), lambda t, eid, used: (l, eid[k * nt + t], 0, 0)),
                pl.BlockSpec((None, None, f, d), lambda t, eid, used: (l, eid[k * nt + t], 0, 0)),
            ] + [row_spec] * len(prev),
            out_specs=row_spec,
        ),
        out_shape=jax.ShapeDtypeStruct((p, d), F32),
        compiler_params=_params("arbitrary"),
        name="experts",
    )(eid, used, xs, w_gate, w_up, w_down, *prev)


def _combine_kernel(pos_ref, x_ref, ys_ref, g_ref, o_ref, y_ref, sem, *, tm, nt, nctx, nb):
    i = pl.program_id(0)
    base = i * tm

    def start(r, carry):
        _row_copy(ys_ref, pos_ref[base + r], y_ref, r, sem).start()
        return carry

    def wait(r, carry):
        _row_copy(ys_ref, pos_ref[base + r], y_ref, r, sem).wait()
        return carry

    lax.fori_loop(0, tm, start, 0, unroll=DMA_UNROLL)
    lax.fori_loop(0, tm, wait, 0, unroll=DMA_UNROLL)
    for s in range(tm // ROW_BLOCK):
        rows = slice(s * ROW_BLOCK, (s + 1) * ROW_BLOCK)
        row = _mod_row(i * (tm // ROW_BLOCK) + s, nt, nctx, nb)
        o_ref[rows, :] = x_ref[rows, :] + g_ref[pl.ds(row, 1), :] * y_ref[rows, :]


def _combine(pos, x2, ys, mod_t, l, geo):
    m, d = x2.shape
    tm = _pick(m, (512, 256))
    kern = functools.partial(_combine_kernel, tm=tm, nt=geo["nt"], nctx=geo["nctx"], nb=geo["nb"])
    return pl.pallas_call(
        kern,
        grid_spec=pltpu.PrefetchScalarGridSpec(
            num_scalar_prefetch=1,
            grid=(m // tm,),
            in_specs=[
                pl.BlockSpec((tm, d), lambda i, pos: (i, 0)),
                pl.BlockSpec(memory_space=pl.ANY),
                pl.BlockSpec((None, None, 16, d), lambda i, pos: (l, 5, 0, 0)),
            ],
            out_specs=pl.BlockSpec((tm, d), lambda i, pos: (i, 0)),
            scratch_shapes=[pltpu.VMEM((tm, d), F32), pltpu.SemaphoreType.DMA(())],
        ),
        out_shape=jax.ShapeDtypeStruct((m, d), F32),
        input_output_aliases={1: 0},
        compiler_params=_params("arbitrary"),
        name="combine",
    )(pos, x2, ys, mod_t)


def _moe_ffn(x2, norm_g, mod_t, wr_hi, wr_lo, rbias, w_gate, w_up, w_down, l, geo):
    m, d = x2.shape
    ne = w_gate.shape[1]
    per = ne // N_GROUPS
    pairs = _pairs(per)
    ncls = N_GROUPS * len(pairs)
    tm = ROW_BLOCK
    nt = m // tm + ncls
    hs, meta, cnt = _ffnprep(x2, norm_g, mod_t, wr_hi, wr_lo, rbias, l, geo, ncls)

    count = cnt[:ncls, 0].astype(jnp.int32)
    tiles = (count + tm - 1) // tm
    tile_end = jnp.cumsum(tiles)
    row0 = (tile_end - tiles) * tm
    pos = row0[meta[0]] + meta[1]
    used = tile_end[-1]
    tidx = jnp.minimum(jnp.arange(nt, dtype=jnp.int32), used - 1)
    tile_cls = jnp.searchsorted(tile_end, tidx, side="right").astype(jnp.int32)
    e_lo = jnp.array([g * per + a for g in range(N_GROUPS) for a, _ in pairs], jnp.int32)
    e_hi = jnp.array([g * per + c for g in range(N_GROUPS) for _, c in pairs], jnp.int32)
    eid = jnp.concatenate([e_lo[tile_cls], e_hi[tile_cls]])

    xs = _dispatch(pos, hs, jnp.zeros((nt * tm, d + LANES), F32))
    used = used.reshape(1)
    ys = _experts(eid, used, xs, w_gate, w_up, w_down, (), l, tm, 0)
    ys = _experts(eid, used, xs, w_gate, w_up, w_down, (ys,), l, tm, 1)
    return _combine(pos, x2, ys, mod_t, l, geo)


def _final_kernel(x_ref, g_ref, o_ref):
    o_ref[...] = _rms(x_ref[...]) * g_ref[...]


def _final_norm(x3, g, s_rows):
    b, lc, d = x3.shape
    tm = ROW_BLOCK
    return pl.pallas_call(
        _final_kernel,
        grid=(b, s_rows // tm),
        in_specs=[
            pl.BlockSpec((None, tm, d), lambda bi, i: (bi, i, 0)),
            pl.BlockSpec((1, d), lambda bi, i: (0, 0)),
        ],
        out_specs=pl.BlockSpec((None, tm, d), lambda bi, i: (bi, i, 0)),
        out_shape=jax.ShapeDtypeStruct((b, s_rows, d), F32),
        compiler_params=_params("arbitrary", "arbitrary"),
        name="final_norm",
    )(x3, g.reshape(1, d))


def _rope_tables(c_rows, s_rows):
    rows = s_rows // GRID_W
    row = jnp.broadcast_to(jnp.arange(rows)[:, None], (rows, GRID_W)).reshape(-1).astype(F32)
    col = jnp.broadcast_to(jnp.arange(GRID_W)[None, :], (rows, GRID_W)).reshape(-1).astype(F32)
    half = QK_ROPE // 2
    inv = ROPE_THETA ** (-jnp.arange(0, half, 2, dtype=F32) / half)
    ang_r = row[:, None] * inv
    ang_c = col[:, None] * inv
    ang = jnp.concatenate([ang_r, ang_r, ang_c, ang_c], axis=-1)
    ang = jnp.concatenate([ang, jnp.zeros((c_rows, QK_ROPE), F32)], axis=0)
    ang = jnp.concatenate([ang, ang], axis=-1)
    lane = jnp.arange(LANES)
    sign = jnp.where((lane % 32) < 16, -1.0, 1.0).astype(F32)
    return jnp.cos(ang), jnp.sin(ang) * sign


def kernel(x, c, ctx, c_ctx, w_mod, b_mod, norm_mix_g, norm_ffn_g, w_in, kv_norm_g, w_ukv, conv_w, conv_b,
           rg_wa, rg_ba, rg_wi, rg_bi, rg_lambda, w_o_mla, w_o_rnn, w_out, w_router, router_bias,
           w_gate, w_up, w_down, final_norm_g):
    b, s_rows, d = x.shape
    c_rows = ctx.shape[1]
    depth = w_mod.shape[0]
    kv_rank = kv_norm_g.shape[-1]
    lc = c_rows + s_rows
    m = b * lc
    assert c_rows % ROW_BLOCK == 0 and s_rows % c_rows == 0 and b < 16
    geo = {"nt": lc // ROW_BLOCK, "nctx": c_rows // ROW_BLOCK, "nb": b}

    nq = N_HEADS * QK_HEAD
    wq = w_in[:, :, :nq].reshape(depth, d, N_HEADS, QK_HEAD)
    off_qr = N_HEADS * QK_NOPE
    off_ckv = off_qr + N_HEADS * QK_ROPE
    off_kr = off_ckv + kv_rank
    off_rx = -(-(off_kr + LANES) // d) * d
    off_ry, off_gm, off_gr = off_rx + d, off_rx + 2 * d, off_rx + 3 * d
    tail = nq + kv_rank + QK_ROPE
    w_in_p = jnp.concatenate([
        wq[..., :QK_NOPE].reshape(depth, d, -1),
        wq[..., QK_NOPE:].reshape(depth, d, -1),
        w_in[:, :, nq:tail],
        jnp.zeros((depth, d, off_rx - off_kr - QK_ROPE), w_in.dtype),
        w_in[:, :, tail:],
    ], axis=-1).astype(BF16)
    wkv = w_ukv.reshape(depth, kv_rank, N_HEADS, QK_NOPE + V_HEAD)
    w_ukv_p = jnp.concatenate([wkv[..., :QK_NOPE].reshape(depth, kv_rank, -1),
                               wkv[..., QK_NOPE:].reshape(depth, kv_rank, -1)], axis=-1).astype(BF16)
    rg_w = jnp.concatenate([rg_wa, rg_wi], axis=-1).astype(BF16)
    rg_p = jnp.stack([rg_ba, rg_bi, rg_lambda], axis=2)
    w_o_mla_b, w_o_rnn_b, w_out_b = w_o_mla.astype(BF16), w_o_rnn.astype(BF16), w_out.astype(BF16)
    w_gate_b, w_up_b, w_down_b = w_gate.astype(BF16), w_up.astype(BF16), w_down.astype(BF16)
    wr_t = w_router.T
    wr_hi = wr_t.astype(BF16)
    wr_lo = (wr_t - wr_hi.astype(F32)).astype(BF16)
    rbias = router_bias.reshape(-1, 1).astype(F32)
    norm_mix = norm_mix_g.reshape(depth, 1, d)
    norm_ffn = norm_ffn_g.reshape(depth, 1, d)
    kv_g = kv_norm_g.reshape(depth, 1, kv_rank)
    conv_b3 = conv_b.reshape(depth, 1, d)
    cos_t, sin_t = _rope_tables(c_rows, s_rows)

    cc = jnp.concatenate([c, c_ctx[None, :], jnp.zeros((16 - b - 1, d), F32)], axis=0)
    mod_t = _mod_table(cc, w_mod, b_mod).reshape(depth, 16, N_MOD, d).transpose(0, 2, 1, 3)

    x2 = jnp.concatenate([x, ctx], axis=1).reshape(m, d)
    for l in range(depth):
        p = _inproj(x2, norm_mix, mod_t, w_in_p, l, geo)
        p3 = p.reshape(b, lc, -1)
        kv = _kvup(p, kv_g, w_ukv_p, l, off_ckv)
        o = _attention(p3, kv.reshape(b, lc, -1), cos_t, sin_t, s_rows, off_qr, off_kr)
        hd = _rglru(p3, conv_w, conv_b3, rg_w, rg_p, l, geo, off_rx)
        z = _merge(o.reshape(m, -1), hd.reshape(2, m, d), p, w_o_mla_b, w_o_rnn_b, l, off_ry, off_gm, off_gr)
        x2 = _outproj(z, w_out_b, x2, mod_t, l, geo)
        x2 = _moe_ffn(x2, norm_ffn, mod_t, wr_hi, wr_lo, rbias, w_gate_b, w_up_b, w_down_b, l, geo)
    return _final_norm(x2.reshape(b, lc, d), final_norm_g, s_rows)
```

```python
import functools

import jax
import jax.numpy as jnp
from jax import lax
from jax.experimental import pallas as pl
from jax.experimental.pallas import tpu as pltpu

N_HEADS = 16
QK_NOPE = 128
QK_ROPE = 64
QK_HEAD = QK_NOPE + QK_ROPE
V_HEAD = 128
GRID_W = 64
ROPE_THETA = 10000.0
RG_BLOCKS = 8
CONV_W = 4
RG_C = 8.0
N_GROUPS = 4
EPS = 1e-6
N_MOD = 6

LANES = 128
SUBLANES = 8
ROW_BLOCK = 256
KEY_CHUNK = 256
DMA_UNROLL = 8
SCAN_PAD = 4
LOG2E = 1.4426950408889634
VMEM_LIMIT = 56 * 1024 * 1024

F32 = jnp.float32
BF16 = jnp.bfloat16


def _pick(n, candidates):
    for c in candidates:
        if n % c == 0:
            return c
    raise ValueError(f"no tile of {candidates} divides {n}")


def _params(*sem):
    return pltpu.CompilerParams(dimension_semantics=sem, vmem_limit_bytes=VMEM_LIMIT)


def _mod_row(blk, nt, nctx, nb):
    return jnp.where(blk % nt >= nt - nctx, nb, blk // nt)


def _rms(x):
    return x * lax.rsqrt(jnp.mean(x * x, axis=-1, keepdims=True) + EPS)


def _mod_kernel(c_ref, w_ref, b_ref, o_ref):
    c = c_ref[...]
    s = (c * jax.nn.sigmoid(c)).astype(BF16)
    o_ref[...] = jnp.dot(s, w_ref[...].astype(BF16), preferred_element_type=F32) + b_ref[...]


def _mod_table(cc, w_mod, b_mod):
    depth, d, n = w_mod.shape
    tn = _pick(n, (1024, 512, 256, 128))
    return pl.pallas_call(
        _mod_kernel,
        grid=(depth, n // tn),
        in_specs=[
            pl.BlockSpec((16, d), lambda l, j: (0, 0)),
            pl.BlockSpec((None, d, tn), lambda l, j: (l, 0, j)),
            pl.BlockSpec((None, 1, tn), lambda l, j: (l, 0, j)),
        ],
        out_specs=pl.BlockSpec((None, 16, tn), lambda l, j: (l, 0, j)),
        out_shape=jax.ShapeDtypeStruct((depth, 16, n), F32),
        compiler_params=_params("arbitrary", "arbitrary"),
        name="mod_table",
    )(cc, w_mod, b_mod.reshape(depth, 1, n))


def _inproj_kernel(x_ref, g_ref, sh_ref, sc_ref, w_ref, o_ref, h_ref, *, tm, nt, nctx, nb):
    i = pl.program_id(0)

    @pl.when(pl.program_id(1) == 0)
    def _():
        for s in range(tm // ROW_BLOCK):
            rows = slice(s * ROW_BLOCK, (s + 1) * ROW_BLOCK)
            row = _mod_row(i * (tm // ROW_BLOCK) + s, nt, nctx, nb)
            y = _rms(x_ref[rows, :]) * g_ref[...]
            y = y * (1.0 + sc_ref[pl.ds(row, 1), :]) + sh_ref[pl.ds(row, 1), :]
            h_ref[rows, :] = y.astype(BF16)

    o_ref[...] = jnp.dot(h_ref[...], w_ref[...], preferred_element_type=F32).astype(BF16)


def _inproj(x2, norm_g, mod_t, w_in_p, l, geo):
    m, d = x2.shape
    n = w_in_p.shape[-1]
    tm = _pick(m, (1024, 512, 256))
    tn = _pick(n, (1024, 512, 256, 128))
    kern = functools.partial(_inproj_kernel, tm=tm, nt=geo["nt"], nctx=geo["nctx"], nb=geo["nb"])
    return pl.pallas_call(
        kern,
        grid=(m // tm, n // tn),
        in_specs=[
            pl.BlockSpec((tm, d), lambda i, j: (i, 0)),
            pl.BlockSpec((None, 1, d), lambda i, j: (l, 0, 0)),
            pl.BlockSpec((None, None, 16, d), lambda i, j: (l, 0, 0, 0)),
            pl.BlockSpec((None, None, 16, d), lambda i, j: (l, 1, 0, 0)),
            pl.BlockSpec((None, d, tn), lambda i, j: (l, 0, j)),
        ],
        out_specs=pl.BlockSpec((tm, tn), lambda i, j: (i, j)),
        out_shape=jax.ShapeDtypeStruct((m, n), BF16),
        scratch_shapes=[pltpu.VMEM((tm, d), BF16)],
        compiler_params=_params("arbitrary", "arbitrary"),
        name="inproj",
    )(x2, norm_g, mod_t, mod_t, w_in_p)


def _kvup_kernel(c_ref, g_ref, w_ref, o_ref, h_ref):
    @pl.when(pl.program_id(1) == 0)
    def _():
        h_ref[...] = (_rms(c_ref[...].astype(F32)) * g_ref[...]).astype(BF16)

    o_ref[...] = jnp.dot(h_ref[...], w_ref[...], preferred_element_type=F32).astype(BF16)


def _kvup(p, kv_g, w_ukv_p, l, off_ckv):
    m = p.shape[0]
    r, n = w_ukv_p.shape[1:]
    tm = _pick(m, (1024, 512, 256))
    tn = _pick(n, (2048, 1024, 512, 256))
    cb = off_ckv // r
    return pl.pallas_call(
        _kvup_kernel,
        grid=(m // tm, n // tn),
        in_specs=[
            pl.BlockSpec((tm, r), lambda i, j: (i, cb)),
            pl.BlockSpec((None, 1, r), lambda i, j: (l, 0, 0)),
            pl.BlockSpec((None, r, tn), lambda i, j: (l, 0, j)),
        ],
        out_specs=pl.BlockSpec((tm, tn), lambda i, j: (i, j)),
        out_shape=jax.ShapeDtypeStruct((m, n), BF16),
        scratch_shapes=[pltpu.VMEM((tm, r), BF16)],
        compiler_params=_params("arbitrary", "arbitrary"),
        name="kvup",
    )(p, kv_g, w_ukv_p)


def _rope(x, cos, sin_signed):
    lane = lax.broadcasted_iota(jnp.int32, x.shape, 1)
    first = (lane % 32) < 16
    rot = jnp.where(first, pltpu.roll(x, LANES - 16, 1), pltpu.roll(x, 16, 1))
    return x * cos + rot * sin_signed


def _key_chunks(n):
    return [(k0, min(KEY_CHUNK, n - k0)) for k0 in range(0, n, KEY_CHUNK)]


def _attn_kernel(qn_ref, qr_ref, kn_ref, kr_ref, v_ref, cos_ref, sin_ref, o_in_ref, o_ref, kcat_ref,
                 *, tq, lc):
    del o_in_ref
    qi = pl.program_id(2)

    @pl.when(qi == 0)
    def _():
        kr = _rope(kr_ref[...].astype(F32), cos_ref[...], sin_ref[...])
        for j in range(2):
            kcat_ref[j, :, 0:QK_NOPE] = kn_ref[:, j * QK_NOPE:(j + 1) * QK_NOPE]
            krj = kr if j == 0 else pltpu.roll(kr, QK_ROPE, 1)
            kcat_ref[j, :, QK_NOPE:QK_NOPE + LANES] = krj.astype(BF16)

    scale = QK_HEAD ** -0.5 * LOG2E
    r0 = pl.multiple_of(qi * tq, tq)
    qr = _rope(qr_ref[...].astype(F32), cos_ref[pl.ds(r0, tq), :], sin_ref[pl.ds(r0, tq), :])
    qr = (qr * scale).astype(BF16)
    qn = (qn_ref[...].astype(F32) * scale).astype(BF16)
    nt_dims = (((1,), (1,)), ((), ()))

    for j in range(2):
        q = jnp.concatenate([qn[:, j * QK_NOPE:(j + 1) * QK_NOPE], qr], axis=1)
        m = part = acc = None
        for k0, kn in _key_chunks(lc):
            s = lax.dot_general(q, kcat_ref[j, k0:k0 + kn, :], nt_dims, preferred_element_type=F32)
            smax = jnp.max(s, axis=-1, keepdims=True)
            m_new = smax if m is None else jnp.maximum(m, smax)
            p = jnp.exp2(s - m_new)
            psum = p[:, 0:LANES]
            for c in range(1, kn // LANES):
                psum = psum + p[:, c * LANES:(c + 1) * LANES]
            pv = jnp.dot(p.astype(BF16), v_ref[k0:k0 + kn, j * V_HEAD:(j + 1) * V_HEAD],
                         preferred_element_type=F32)
            if m is None:
                part, acc = psum, pv
            else:
                alpha = jnp.exp2(m - m_new)
                part = alpha * part + psum
                acc = alpha * acc + pv
            m = m_new
        den = jnp.sum(part, axis=-1, keepdims=True)
        o_ref[:, j * V_HEAD:(j + 1) * V_HEAD] = (acc / den).astype(BF16)


def _attn_ctx_kernel(qn_ref, qr_ref, kn_ref, kr_ref, v_ref, o_in_ref, o_ref):
    del o_in_ref
    scale = QK_HEAD ** -0.5
    qr = (qr_ref[...].astype(F32) * scale).astype(BF16)
    qn = (qn_ref[...].astype(F32) * scale).astype(BF16)
    kr = kr_ref[...].astype(F32)
    for j in range(2):
        krj = kr if j == 0 else pltpu.roll(kr, QK_ROPE, 1)
        k = jnp.concatenate([kn_ref[:, j * QK_NOPE:(j + 1) * QK_NOPE], krj.astype(BF16)], axis=1)
        q = jnp.concatenate([qn[:, j * QK_NOPE:(j + 1) * QK_NOPE], qr], axis=1)
        s = lax.dot_general(q, k, (((1,), (1,)), ((), ())), preferred_element_type=F32)
        p = jnp.exp(s - jnp.max(s, axis=-1, keepdims=True))
        den = jnp.sum(p, axis=-1, keepdims=True)
        o = jnp.dot(p.astype(BF16), v_ref[:, j * V_HEAD:(j + 1) * V_HEAD], preferred_element_type=F32)
        o_ref[:, j * V_HEAD:(j + 1) * V_HEAD] = (o / den).astype(BF16)


def _attention(p3, kv3, cos_t, sin_t, s_rows, off_qr, off_kr):
    b, lc, _ = p3.shape
    c_rows = lc - s_rows
    hp = N_HEADS // 2
    tq = _pick(s_rows, (512, 256))
    qrb = off_qr // LANES
    krb = off_kr // LANES
    o = pl.pallas_call(
        functools.partial(_attn_kernel, tq=tq, lc=lc),
        grid=(b, hp, s_rows // tq),
        in_specs=[
            pl.BlockSpec((None, tq, 2 * QK_NOPE), lambda bi, h, q: (bi, q, h)),
            pl.BlockSpec((None, tq, LANES), lambda bi, h, q: (bi, q, qrb + h)),
            pl.BlockSpec((None, lc, 2 * QK_NOPE), lambda bi, h, q: (bi, 0, h)),
            pl.BlockSpec((None, lc, LANES), lambda bi, h, q: (bi, 0, krb)),
            pl.BlockSpec((None, lc, 2 * V_HEAD), lambda bi, h, q: (bi, 0, hp + h)),
            pl.BlockSpec((lc, LANES), lambda bi, h, q: (0, 0)),
            pl.BlockSpec((lc, LANES), lambda bi, h, q: (0, 0)),
            pl.BlockSpec(memory_space=pl.ANY),
        ],
        out_specs=pl.BlockSpec((None, tq, 2 * V_HEAD), lambda bi, h, q: (bi, q, h)),
        out_shape=jax.ShapeDtypeStruct((b, lc, N_HEADS * V_HEAD), BF16),
        scratch_shapes=[pltpu.VMEM((2, lc, QK_NOPE + LANES), BF16)],
        input_output_aliases={7: 0},
        compiler_params=_params("arbitrary", "arbitrary", "arbitrary"),
        name="attention",
    )(p3, p3, kv3, p3, kv3, cos_t, sin_t, jnp.zeros((b, lc, N_HEADS * V_HEAD), BF16))
    cb = s_rows // c_rows
    return pl.pallas_call(
        _attn_ctx_kernel,
        grid=(b, hp),
        in_specs=[
            pl.BlockSpec((None, c_rows, 2 * QK_NOPE), lambda bi, h: (bi, cb, h)),
            pl.BlockSpec((None, c_rows, LANES), lambda bi, h: (bi, cb, qrb + h)),
            pl.BlockSpec((None, c_rows, 2 * QK_NOPE), lambda bi, h: (bi, cb, h)),
            pl.BlockSpec((None, c_rows, LANES), lambda bi, h: (bi, cb, krb)),
            pl.BlockSpec((None, c_rows, 2 * V_HEAD), lambda bi, h: (bi, cb, hp + h)),
            pl.BlockSpec(memory_space=pl.ANY),
        ],
        out_specs=pl.BlockSpec((None, c_rows, 2 * V_HEAD), lambda bi, h: (bi, cb, h)),
        out_shape=jax.ShapeDtypeStruct(o.shape, BF16),
        input_output_aliases={5: 0},
        compiler_params=_params("arbitrary", "arbitrary"),
        name="attention_ctx",
    )(p3, p3, kv3, p3, kv3, o)


def _sigmoid(x):
    return 0.5 * jnp.tanh(0.5 * x) + 0.5


def _scan_tile(a_ref, b_ref, h_ref, o_ref, *, seg, pitch, reverse):
    nslab = a_ref.shape[0]

    def body(g, carry):
        row = (seg - 1 - g) if reverse else g
        idx = pl.ds(row, SUBLANES, stride=pitch)
        out = []
        for c in range(nslab):
            av = a_ref[c, idx, :]
            hc = av * carry[2 * c] + b_ref[c, idx, :]
            ac = av * carry[2 * c + 1]
            b_ref[c, idx, :] = hc
            a_ref[c, idx, :] = ac
            out += [hc, ac]
        return tuple(out)

    init = (jnp.zeros((SUBLANES, LANES), F32), jnp.ones((SUBLANES, LANES), F32)) * nslab
    fin = lax.fori_loop(0, seg, body, init)
    sub = lax.broadcasted_iota(jnp.int32, (SUBLANES, LANES), 0)
    for c in range(nslab):
        lanes = slice(c * LANES, (c + 1) * LANES)
        h_end, a_end = fin[2 * c], fin[2 * c + 1]
        cin = h_ref[:, lanes]
        for s in (range(SUBLANES - 1, 0, -1) if reverse else range(SUBLANES - 1)):
            nxt = a_end * cin + h_end
            if reverse:
                cin = jnp.where(sub == s - 1, pltpu.roll(nxt, SUBLANES - 1, 0), cin)
            else:
                cin = jnp.where(sub == s + 1, pltpu.roll(nxt, 1, 0), cin)
        out = a_end * cin + h_end
        edge = out[0:1, :] if reverse else out[SUBLANES - 1:SUBLANES, :]
        h_ref[:, lanes] = jnp.broadcast_to(edge, (SUBLANES, LANES))
        for s in range(SUBLANES):
            rows = slice(s * pitch, s * pitch + seg)
            h = b_ref[c, rows, :] + a_ref[c, rows, :] * cin[s:s + 1, :]
            o_ref[s * seg:(s + 1) * seg, lanes] = h.astype(BF16)


def _rglru_kernel(x_ref, xp_ref, xn_ref, cw_ref, cb_ref, w_ref, rp_ref, o_ref, xe_ref, a_ref, b_ref, h_ref,
                  *, tt, nt, nctx, d, bw):
    dr = pl.program_id(0)
    i = pl.program_id(2)
    t = _tile_of(dr, i, nt, nctx)
    nlat = nt - nctx
    seg = tt // SUBLANES
    pitch = a_ref.shape[1] // SUBLANES
    first = jnp.logical_or(t == 0, t == nlat)
    last = jnp.logical_or(t == nlat - 1, t == nt - 1)

    @pl.when(i == 0)
    def _():
        h_ref[...] = jnp.zeros_like(h_ref)

    xe_ref[0:SUBLANES, :] = jnp.where(first, 0.0, xp_ref[...].astype(F32)[SUBLANES:, :])
    xe_ref[SUBLANES:SUBLANES + tt, :] = x_ref[...].astype(F32)
    xe_ref[SUBLANES + tt:, :] = jnp.where(last, 0.0, xn_ref[...].astype(F32)[:SUBLANES, :])
    xr = cb_ref[...]
    for k in range(CONV_W):
        xr = xr + xe_ref[SUBLANES - 2 + k:SUBLANES - 2 + k + tt, :] * cw_ref[k:k + 1, :]
    xb = xr.astype(BF16)

    ba = rp_ref[0:1, :]
    bi = rp_ref[1:2, :]
    sp = jax.nn.softplus(-rp_ref[2:3, :])
    for n in range(d // bw):
        cols = slice(n * bw, (n + 1) * bw)
        g = jnp.dot(xb[:, cols], w_ref[n], preferred_element_type=F32)
        r = _sigmoid(g[:, :bw] + ba[:, cols])
        ig = _sigmoid(g[:, bw:] + bi[:, cols])
        log_a = (-RG_C) * r * sp[:, cols]
        a = jnp.exp(log_a)
        bx = jnp.sqrt(1.0 - a * a) * (ig * xr[:, cols])
        for c in range(bw // LANES):
            slab = n * (bw // LANES) + c
            lanes = slice(c * LANES, (c + 1) * LANES)
            for s in range(SUBLANES):
                a_ref[slab, s * pitch:s * pitch + seg, :] = a[s * seg:(s + 1) * seg, lanes]
                b_ref[slab, s * pitch:s * pitch + seg, :] = bx[s * seg:(s + 1) * seg, lanes]

    @pl.when(dr == 0)
    def _():
        _scan_tile(a_ref, b_ref, h_ref, o_ref, seg=seg, pitch=pitch, reverse=False)

    @pl.when(dr == 1)
    def _():
        _scan_tile(a_ref, b_ref, h_ref, o_ref, seg=seg, pitch=pitch, reverse=True)


def _tile_of(dr, i, nt, nctx):
    nlat = nt - nctx
    fwd = jnp.where(i < nctx, nlat + i, i - nctx)
    back = jnp.where(i < nctx, nt - 1 - i, nlat - 1 - (i - nctx))
    return jnp.where(dr == 0, fwd, back)


def _rglru(p3, conv_w, conv_b, rg_w, rg_p, l, geo, off_rx):
    b, lc, _ = p3.shape
    d = conv_w.shape[-1]
    bw = d // RG_BLOCKS
    tt = ROW_BLOCK
    nt = lc // tt
    nctx = geo["nctx"]
    xb = off_rx // d
    hb = tt // 16
    nh = lc // 16

    def tile(dr, i):
        return _tile_of(dr, i, nt, nctx)

    kern = functools.partial(_rglru_kernel, tt=tt, nt=nt, nctx=nctx, d=d, bw=bw)
    return pl.pallas_call(
        kern,
        grid=(2, b, nt),
        in_specs=[
            pl.BlockSpec((None, tt, d), lambda dr, bi, i: (bi, tile(dr, i), xb)),
            pl.BlockSpec((None, 16, d), lambda dr, bi, i: (bi, jnp.maximum(tile(dr, i) * hb - 1, 0), xb)),
            pl.BlockSpec((None, 16, d), lambda dr, bi, i: (bi, jnp.minimum((tile(dr, i) + 1) * hb, nh - 1), xb)),
            pl.BlockSpec((None, CONV_W, d), lambda dr, bi, i: (l, 0, 0)),
            pl.BlockSpec((None, 1, d), lambda dr, bi, i: (l, 0, 0)),
            pl.BlockSpec((None, None, RG_BLOCKS, bw, 2 * bw), lambda dr, bi, i: (l, dr, 0, 0, 0)),
            pl.BlockSpec((None, None, 3, d), lambda dr, bi, i: (l, dr, 0, 0)),
        ],
        out_specs=pl.BlockSpec((None, None, tt, d), lambda dr, bi, i: (dr, bi, tile(dr, i), 0)),
        out_shape=jax.ShapeDtypeStruct((2, b, lc, d), BF16),
        scratch_shapes=[
            pltpu.VMEM((tt + 2 * SUBLANES, d), F32),
            pltpu.VMEM((d // LANES, tt + SUBLANES * SCAN_PAD, LANES), F32),
            pltpu.VMEM((d // LANES, tt + SUBLANES * SCAN_PAD, LANES), F32),
            pltpu.VMEM((SUBLANES, d), F32),
        ],
        compiler_params=_params("arbitrary", "arbitrary", "arbitrary"),
        name="rglru",
    )(p3, p3, p3, conv_w, conv_b, rg_w, rg_p)


def _merge_kernel(o_ref, hf_ref, hb_ref, ry_ref, gm_ref, gr_ref, wm_ref, wr_ref, z_ref, r_ref):
    @pl.when(pl.program_id(1) == 0)
    def _():
        h = hf_ref[...].astype(F32) + hb_ref[...].astype(F32)
        r_ref[...] = (h * jax.nn.gelu(ry_ref[...].astype(F32))).astype(BF16)

    ym = jnp.dot(o_ref[...], wm_ref[...], preferred_element_type=F32)
    yr = jnp.dot(r_ref[...], wr_ref[...], preferred_element_type=F32)
    z = jax.nn.sigmoid(gm_ref[...].astype(F32)) * ym + jax.nn.sigmoid(gr_ref[...].astype(F32)) * yr
    z_ref[...] = z.astype(BF16)


def _merge(o2, h2d, p, w_o_mla, w_o_rnn, l, off_ry, off_gm, off_gr):
    m, d = o2.shape
    tm = ROW_BLOCK
    tn = d
    ryb = off_ry // d
    gmb = off_gm // tn
    grb = off_gr // tn
    return pl.pallas_call(
        _merge_kernel,
        grid=(m // tm, d // tn),
        in_specs=[
            pl.BlockSpec((tm, d), lambda i, j: (i, 0)),
            pl.BlockSpec((None, tm, d), lambda i, j: (0, i, 0)),
            pl.BlockSpec((None, tm, d), lambda i, j: (1, i, 0)),
            pl.BlockSpec((tm, d), lambda i, j: (i, ryb)),
            pl.BlockSpec((tm, tn), lambda i, j: (i, gmb + j)),
            pl.BlockSpec((tm, tn), lambda i, j: (i, grb + j)),
            pl.BlockSpec((None, d, tn), lambda i, j: (l, 0, j)),
            pl.BlockSpec((None, d, tn), lambda i, j: (l, 0, j)),
        ],
        out_specs=pl.BlockSpec((tm, tn), lambda i, j: (i, j)),
        out_shape=jax.ShapeDtypeStruct((m, d), BF16),
        scratch_shapes=[pltpu.VMEM((tm, d), BF16)],
        compiler_params=_params("arbitrary", "arbitrary"),
        name="merge",
    )(o2, h2d, h2d, p, p, p, w_o_mla, w_o_rnn)


def _outproj_kernel(z_ref, w_ref, x_ref, g_ref, o_ref, *, tm, nt, nctx, nb):
    i = pl.program_id(0)
    y = jnp.dot(z_ref[...], w_ref[...], preferred_element_type=F32)
    for s in range(tm // ROW_BLOCK):
        rows = slice(s * ROW_BLOCK, (s + 1) * ROW_BLOCK)
        row = _mod_row(i * (tm // ROW_BLOCK) + s, nt, nctx, nb)
        o_ref[rows, :] = x_ref[rows, :] + g_ref[pl.ds(row, 1), :] * y[rows, :]


def _outproj(z, w_out, x2, mod_t, l, geo):
    m, d = x2.shape
    tm = _pick(m, (1024, 512, 256))
    tn = _pick(d, (512, 256))
    kern = functools.partial(_outproj_kernel, tm=tm, nt=geo["nt"], nctx=geo["nctx"], nb=geo["nb"])
    return pl.pallas_call(
        kern,
        grid=(m // tm, d // tn),
        in_specs=[
            pl.BlockSpec((tm, d), lambda i, j: (i, 0)),
            pl.BlockSpec((None, d, tn), lambda i, j: (l, 0, j)),
            pl.BlockSpec((tm, tn), lambda i, j: (i, j)),
            pl.BlockSpec((None, None, 16, tn), lambda i, j: (l, 2, 0, j)),
        ],
        out_specs=pl.BlockSpec((tm, tn), lambda i, j: (i, j)),
        out_shape=jax.ShapeDtypeStruct((m, d), F32),
        input_output_aliases={2: 0},
        compiler_params=_params("arbitrary", "arbitrary"),
        name="outproj",
    )(z, w_out, x2, mod_t)


def _pairs(per):
    return [(a, c) for a in range(per) for c in range(a + 1, per)]


def _router(h, wr_hi, wr_lo, rb):
    hi = h.astype(BF16)
    lo = (h - hi.astype(F32)).astype(BF16)
    nt_dims = (((1,), (1,)), ((), ()))
    logits = (lax.dot_general(wr_hi, hi, nt_dims, preferred_element_type=F32)
              + lax.dot_general(wr_hi, lo, nt_dims, preferred_element_type=F32)
              + lax.dot_general(wr_lo, hi, nt_dims, preferred_element_type=F32))
    scores = jax.nn.sigmoid(logits)
    biased = scores + rb
    e = scores.shape[0]
    per = e // N_GROUPS
    rows_b = [biased[j:j + 1, :] for j in range(e)]
    rows_s = [scores[j:j + 1, :] for j in range(e)]
    gscore = []
    for g in range(N_GROUPS):
        r = rows_b[g * per:(g + 1) * per]
        best = None
        for a in range(per):
            for c in range(a + 1, per):
                pair = r[a] + r[c]
                best = pair if best is None else jnp.maximum(best, pair)
        gscore.append(best)
    gbest = gscore[0]
    gidx = jnp.zeros_like(gbest, dtype=jnp.int32)
    for g in range(1, N_GROUPS):
        better = gscore[g] > gbest
        gbest = jnp.where(better, gscore[g], gbest)
        gidx = jnp.where(better, g, gidx)
    sel = []
    for g in range(N_GROUPS):
        r = rows_b[g * per:(g + 1) * per]
        for a in range(per):
            rank = jnp.zeros_like(gidx)
            for c in range(per):
                if c == a:
                    continue
                ahead = (r[c] > r[a]) if c > a else (r[c] >= r[a])
                rank = rank + ahead.astype(jnp.int32)
            sel.append(jnp.logical_and(gidx == g, rank < 2))
    den = None
    for j in range(e):
        term = jnp.where(sel[j], rows_s[j], 0.0)
        den = term if den is None else den + term
    cls = jnp.zeros_like(gidx)
    w_lo = jnp.zeros_like(den)
    w_hi = jnp.zeros_like(den)
    pairs = _pairs(per)
    for g in range(N_GROUPS):
        for pi, (a, c) in enumerate(pairs):
            both = jnp.logical_and(sel[g * per + a], sel[g * per + c])
            cls = jnp.where(both, g * len(pairs) + pi, cls)
            w_lo = jnp.where(both, rows_s[g * per + a] / den, w_lo)
            w_hi = jnp.where(both, rows_s[g * per + c] / den, w_hi)
    return cls, w_lo, w_hi


def _ffnprep_kernel(x_ref, g_ref, sh_ref, sc_ref, wh_ref, wl_ref, rb_ref, hs_ref, meta_ref, cnt_ref, run_ref,
                    *, tm, nt, nctx, nb, d):
    i = pl.program_id(0)

    @pl.when(i == 0)
    def _():
        run_ref[...] = jnp.zeros_like(run_ref)

    hs = []
    for s in range(tm // ROW_BLOCK):
        rows = slice(s * ROW_BLOCK, (s + 1) * ROW_BLOCK)
        row = _mod_row(i * (tm // ROW_BLOCK) + s, nt, nctx, nb)
        y = _rms(x_ref[rows, :]) * g_ref[...]
        y = y * (1.0 + sc_ref[pl.ds(row, 1), :]) + sh_ref[pl.ds(row, 1), :]
        hs_ref[rows, 0:d] = y
        hs.append(y)
    h = jnp.concatenate(hs, axis=0) if len(hs) > 1 else hs[0]
    cls, w_lo, w_hi = _router(h, wh_ref[...], wl_ref[...], rb_ref[...])
    extra = jnp.concatenate([w_lo, w_hi, jnp.zeros((LANES - 2, tm), F32)], axis=0)
    hs_ref[:, d:d + LANES] = jnp.transpose(extra)

    nc = run_ref.shape[0]
    onehot = (lax.broadcasted_iota(jnp.int32, (nc, tm), 0) == cls).astype(F32)
    before = lax.broadcasted_iota(jnp.int32, (tm, tm), 0) < lax.broadcasted_iota(jnp.int32, (tm, tm), 1)
    prefix = jnp.dot(onehot.astype(BF16), before.astype(BF16), preferred_element_type=F32)
    run = run_ref[:, 0:1]
    rank = jnp.sum(onehot * (prefix + run), axis=0, keepdims=True).astype(jnp.int32)
    meta_ref[...] = jnp.concatenate([cls, rank, jnp.zeros((SUBLANES - 2, tm), jnp.int32)], axis=0)
    run_ref[...] = run_ref[...] + jnp.sum(onehot, axis=1, keepdims=True)
    cnt_ref[...] = run_ref[...]


def _ffnprep(x2, norm_g, mod_t, wr_hi, wr_lo, rbias, l, geo, ncls):
    m, d = x2.shape
    e = wr_hi.shape[0]
    tm = _pick(m, (512, 256))
    nc = -(-ncls // SUBLANES) * SUBLANES
    kern = functools.partial(_ffnprep_kernel, tm=tm, nt=geo["nt"], nctx=geo["nctx"], nb=geo["nb"], d=d)
    return pl.pallas_call(
        kern,
        grid=(m // tm,),
        in_specs=[
            pl.BlockSpec((tm, d), lambda i: (i, 0)),
            pl.BlockSpec((None, 1, d), lambda i: (l, 0, 0)),
            pl.BlockSpec((None, None, 16, d), lambda i: (l, 3, 0, 0)),
            pl.BlockSpec((None, None, 16, d), lambda i: (l, 4, 0, 0)),
            pl.BlockSpec((e, d), lambda i: (0, 0)),
            pl.BlockSpec((e, d), lambda i: (0, 0)),
            pl.BlockSpec((e, 1), lambda i: (0, 0)),
        ],
        out_specs=[
            pl.BlockSpec((tm, d + LANES), lambda i: (i, 0)),
            pl.BlockSpec((SUBLANES, tm), lambda i: (0, i)),
            pl.BlockSpec((nc, LANES), lambda i: (0, 0)),
        ],
        out_shape=[jax.ShapeDtypeStruct((m, d + LANES), F32), jax.ShapeDtypeStruct((SUBLANES, m), jnp.int32),
                   jax.ShapeDtypeStruct((nc, LANES), F32)],
        scratch_shapes=[pltpu.VMEM((nc, LANES), F32)],
        compiler_params=_params("arbitrary"),
        name="ffnprep",
    )(x2, norm_g, mod_t, mod_t, wr_hi, wr_lo, rbias)


def _row_copy(src_ref, src_row, dst_ref, dst_row, sem):
    return pltpu.make_async_copy(src_ref.at[pl.ds(src_row, 1)], dst_ref.at[pl.ds(dst_row, 1)], sem)


def _dispatch_kernel(pos_ref, hs_ref, xs_in_ref, xs_ref, sem, *, tm):
    del xs_in_ref
    base = pl.program_id(0) * tm

    def start(r, carry):
        _row_copy(hs_ref, r, xs_ref, pos_ref[base + r], sem).start()
        return carry

    def wait(r, carry):
        _row_copy(hs_ref, r, xs_ref, pos_ref[base + r], sem).wait()
        return carry

    lax.fori_loop(0, tm, start, 0, unroll=DMA_UNROLL)
    lax.fori_loop(0, tm, wait, 0, unroll=DMA_UNROLL)


def _dispatch(pos, hs, xs0):
    m, w = hs.shape
    tm = _pick(m, (512, 256))
    return pl.pallas_call(
        functools.partial(_dispatch_kernel, tm=tm),
        grid_spec=pltpu.PrefetchScalarGridSpec(
            num_scalar_prefetch=1,
            grid=(m // tm,),
            in_specs=[
                pl.BlockSpec((tm, w), lambda i, pos: (i, 0)),
                pl.BlockSpec(memory_space=pl.ANY),
            ],
            out_specs=pl.BlockSpec(memory_space=pl.ANY),
            scratch_shapes=[pltpu.SemaphoreType.DMA(())],
        ),
        out_shape=jax.ShapeDtypeStruct(xs0.shape, F32),
        input_output_aliases={2: 0},
        compiler_params=_params("arbitrary"),
        name="dispatch",
    )(pos, hs, xs0)


def _expert_kernel(eid_ref, used_ref, xs_ref, wg_ref, wu_ref, wd_ref, *rest, d, k):
    del eid_ref
    o_ref = rest[-1]
    t = pl.program_id(0)

    @pl.when(t < used_ref[0])
    def _():
        x = xs_ref[:, 0:d].astype(BF16)
        gate = jnp.dot(x, wg_ref[...], preferred_element_type=F32)
        up = jnp.dot(x, wu_ref[...], preferred_element_type=F32)
        act = (gate * jax.nn.sigmoid(gate) * up).astype(BF16)
        y = jnp.dot(act, wd_ref[...], preferred_element_type=F32)
        y = xs_ref[:, d + k:d + k + 1] * y
        o_ref[...] = y if k == 0 else rest[0][...] + y

    @pl.when(t >= used_ref[0])
    def _():
        o_ref[...] = jnp.zeros_like(o_ref)


def _experts(eid, used, xs, w_gate, w_up, w_down, prev, l, tm, k):
    p, w = xs.shape
    d = w - LANES
    f = w_gate.shape[-1]
    nt = p // tm
    row_spec = pl.BlockSpec((tm, d), lambda t, eid, used: (t, 0))
    return pl.pallas_call(
        functools.partial(_expert_kernel, d=d, k=k),
        grid_spec=pltpu.PrefetchScalarGridSpec(
            num_scalar_prefetch=2,
            grid=(nt,),
            in_specs=[
                pl.BlockSpec((tm, w), lambda t, eid, used: (t, 0)),
                pl.BlockSpec((None, None, d, f), lambda t, eid, used: (l, eid[k * nt + t], 0, 0)),
                pl.BlockSpec((None, None, d, f), lambda t, eid, used: (l, eid[k * nt + t], 0, 0)),
                pl.BlockSpec((None, None, f, d), lambda t, eid, used: (l, eid[k * nt + t], 0, 0)),
            ] + [row_spec] * len(prev),
            out_specs=row_spec,
        ),
        out_shape=jax.ShapeDtypeStruct((p, d), F32),
        compiler_params=_params("arbitrary"),
        name="experts",
    )(eid, used, xs, w_gate, w_up, w_down, *prev)


def _combine_kernel(pos_ref, x_ref, ys_ref, g_ref, o_ref, y_ref, sem, *, tm, nt, nctx, nb):
    i = pl.program_id(0)
    base = i * tm

    def start(r, carry):
        _row_copy(ys_ref, pos_ref[base + r], y_ref, r, sem).start()
        return carry

    def wait(r, carry):
        _row_copy(ys_ref, pos_ref[base + r], y_ref, r, sem).wait()
        return carry

    lax.fori_loop(0, tm, start, 0, unroll=DMA_UNROLL)
    lax.fori_loop(0, tm, wait, 0, unroll=DMA_UNROLL)
    for s in range(tm // ROW_BLOCK):
        rows = slice(s * ROW_BLOCK, (s + 1) * ROW_BLOCK)
        row = _mod_row(i * (tm // ROW_BLOCK) + s, nt, nctx, nb)
        o_ref[rows, :] = x_ref[rows, :] + g_ref[pl.ds(row, 1), :] * y_ref[rows, :]


def _combine(pos, x2, ys, mod_t, l, geo):
    m, d = x2.shape
    tm = _pick(m, (512, 256))
    kern = functools.partial(_combine_kernel, tm=tm, nt=geo["nt"], nctx=geo["nctx"], nb=geo["nb"])
    return pl.pallas_call(
        kern,
        grid_spec=pltpu.PrefetchScalarGridSpec(
            num_scalar_prefetch=1,
            grid=(m // tm,),
            in_specs=[
                pl.BlockSpec((tm, d), lambda i, pos: (i, 0)),
                pl.BlockSpec(memory_space=pl.ANY),
                pl.BlockSpec((None, None, 16, d), lambda i, pos: (l, 5, 0, 0)),
            ],
            out_specs=pl.BlockSpec((tm, d), lambda i, pos: (i, 0)),
            scratch_shapes=[pltpu.VMEM((tm, d), F32), pltpu.SemaphoreType.DMA(())],
        ),
        out_shape=jax.ShapeDtypeStruct((m, d), F32),
        input_output_aliases={1: 0},
        compiler_params=_params("arbitrary"),
        name="combine",
    )(pos, x2, ys, mod_t)


def _moe_ffn(x2, norm_g, mod_t, wr_hi, wr_lo, rbias, w_gate, w_up, w_down, l, geo):
    m, d = x2.shape
    ne = w_gate.shape[1]
    per = ne // N_GROUPS
    pairs = _pairs(per)
    ncls = N_GROUPS * len(pairs)
    tm = ROW_BLOCK
    nt = m // tm + ncls
    hs, meta, cnt = _ffnprep(x2, norm_g, mod_t, wr_hi, wr_lo, rbias, l, geo, ncls)

    count = cnt[:ncls, 0].astype(jnp.int32)
    tiles = (count + tm - 1) // tm
    tile_end = jnp.cumsum(tiles)
    row0 = (tile_end - tiles) * tm
    pos = row0[meta[0]] + meta[1]
    used = tile_end[-1]
    tidx = jnp.minimum(jnp.arange(nt, dtype=jnp.int32), used - 1)
    tile_cls = jnp.searchsorted(tile_end, tidx, side="right").astype(jnp.int32)
    e_lo = jnp.array([g * per + a for g in range(N_GROUPS) for a, _ in pairs], jnp.int32)
    e_hi = jnp.array([g * per + c for g in range(N_GROUPS) for _, c in pairs], jnp.int32)
    eid = jnp.concatenate([e_lo[tile_cls], e_hi[tile_cls]])

    xs = _dispatch(pos, hs, jnp.zeros((nt * tm, d + LANES), F32))
    used = used.reshape(1)
    ys = _experts(eid, used, xs, w_gate, w_up, w_down, (), l, tm, 0)
    ys = _experts(eid, used, xs, w_gate, w_up, w_down, (ys,), l, tm, 1)
    return _combine(pos, x2, ys, mod_t, l, geo)


def _final_kernel(x_ref, g_ref, o_ref):
    o_ref[...] = _rms(x_ref[...]) * g_ref[...]


def _final_norm(x3, g, s_rows):
    b, lc, d = x3.shape
    tm = ROW_BLOCK
    return pl.pallas_call(
        _final_kernel,
        grid=(b, s_rows // tm),
        in_specs=[
            pl.BlockSpec((None, tm, d), lambda bi, i: (bi, i, 0)),
            pl.BlockSpec((1, d), lambda bi, i: (0, 0)),
        ],
        out_specs=pl.BlockSpec((None, tm, d), lambda bi, i: (bi, i, 0)),
        out_shape=jax.ShapeDtypeStruct((b, s_rows, d), F32),
        compiler_params=_params("arbitrary", "arbitrary"),
        name="final_norm",
    )(x3, g.reshape(1, d))


def _rope_tables(c_rows, s_rows):
    rows = s_rows // GRID_W
    row = jnp.broadcast_to(jnp.arange(rows)[:, None], (rows, GRID_W)).reshape(-1).astype(F32)
    col = jnp.broadcast_to(jnp.arange(GRID_W)[None, :], (rows, GRID_W)).reshape(-1).astype(F32)
    half = QK_ROPE // 2
    inv = ROPE_THETA ** (-jnp.arange(0, half, 2, dtype=F32) / half)
    ang_r = row[:, None] * inv
    ang_c = col[:, None] * inv
    ang = jnp.concatenate([ang_r, ang_r, ang_c, ang_c], axis=-1)
    ang = jnp.concatenate([ang, jnp.zeros((c_rows, QK_ROPE), F32)], axis=0)
    ang = jnp.concatenate([ang, ang], axis=-1)
    lane = jnp.arange(LANES)
    sign = jnp.where((lane % 32) < 16, -1.0, 1.0).astype(F32)
    return jnp.cos(ang), jnp.sin(ang) * sign


def kernel(x, c, ctx, c_ctx, w_mod, b_mod, norm_mix_g, norm_ffn_g, w_in, kv_norm_g, w_ukv, conv_w, conv_b,
           rg_wa, rg_ba, rg_wi, rg_bi, rg_lambda, w_o_mla, w_o_rnn, w_out, w_router, router_bias,
           w_gate, w_up, w_down, final_norm_g):
    b, s_rows, d = x.shape
    c_rows = ctx.shape[1]
    depth = w_mod.shape[0]
    kv_rank = kv_norm_g.shape[-1]
    lc = c_rows + s_rows
    m = b * lc
    assert c_rows % ROW_BLOCK == 0 and s_rows % c_rows == 0 and b < 16
    geo = {"nt": lc // ROW_BLOCK, "nctx": c_rows // ROW_BLOCK, "nb": b}

    nq = N_HEADS * QK_HEAD
    wq = w_in[:, :, :nq].reshape(depth, d, N_HEADS, QK_HEAD)
    off_qr = N_HEADS * QK_NOPE
    off_ckv = off_qr + N_HEADS * QK_ROPE
    off_kr = off_ckv + kv_rank
    off_rx = -(-(off_kr + LANES) // d) * d
    off_ry, off_gm, off_gr = off_rx + d, off_rx + 2 * d, off_rx + 3 * d
    tail = nq + kv_rank + QK_ROPE
    w_in_p = jnp.concatenate([
        wq[..., :QK_NOPE].reshape(depth, d, -1),
        wq[..., QK_NOPE:].reshape(depth, d, -1),
        w_in[:, :, nq:tail],
        jnp.zeros((depth, d, off_rx - off_kr - QK_ROPE), w_in.dtype),
        w_in[:, :, tail:],
    ], axis=-1).astype(BF16)
    wkv = w_ukv.reshape(depth, kv_rank, N_HEADS, QK_NOPE + V_HEAD)
    w_ukv_p = jnp.concatenate([wkv[..., :QK_NOPE].reshape(depth, kv_rank, -1),
                               wkv[..., QK_NOPE:].reshape(depth, kv_rank, -1)], axis=-1).astype(BF16)
    rg_w = jnp.concatenate([rg_wa, rg_wi], axis=-1).astype(BF16)
    rg_p = jnp.stack([rg_ba, rg_bi, rg_lambda], axis=2)
    w_o_mla_b, w_o_rnn_b, w_out_b = w_o_mla.astype(BF16), w_o_rnn.astype(BF16), w_out.astype(BF16)
    w_gate_b, w_up_b, w_down_b = w_gate.astype(BF16), w_up.astype(BF16), w_down.astype(BF16)
    wr_t = w_router.T
    wr_hi = wr_t.astype(BF16)
    wr_lo = (wr_t - wr_hi.astype(F32)).astype(BF16)
    rbias = router_bias.reshape(-1, 1).astype(F32)
    norm_mix = norm_mix_g.reshape(depth, 1, d)
    norm_ffn = norm_ffn_g.reshape(depth, 1, d)
    kv_g = kv_norm_g.reshape(depth, 1, kv_rank)
    conv_b3 = conv_b.reshape(depth, 1, d)
    cos_t, sin_t = _rope_tables(c_rows, s_rows)

    cc = jnp.concatenate([c, c_ctx[None, :], jnp.zeros((16 - b - 1, d), F32)], axis=0)
    mod_t = _mod_table(cc, w_mod, b_mod).reshape(depth, 16, N_MOD, d).transpose(0, 2, 1, 3)

    x2 = jnp.concatenate([x, ctx], axis=1).reshape(m, d)
    for l in range(depth):
        p = _inproj(x2, norm_mix, mod_t, w_in_p, l, geo)
        p3 = p.reshape(b, lc, -1)
        kv = _kvup(p, kv_g, w_ukv_p, l, off_ckv)
        o = _attention(p3, kv.reshape(b, lc, -1), cos_t, sin_t, s_rows, off_qr, off_kr)
        hd = _rglru(p3, conv_w, conv_b3, rg_w, rg_p, l, geo, off_rx)
        z = _merge(o.reshape(m, -1), hd.reshape(2, m, d), p, w_o_mla_b, w_o_rnn_b, l, off_ry, off_gm, off_gr)
        x2 = _outproj(z, w_out_b, x2, mod_t, l, geo)
        x2 = _moe_ffn(x2, norm_ffn, mod_t, wr_hi, wr_lo, rbias, w_gate_b, w_up_b, w_down_b, l, geo)
    return _final_norm(x2.reshape(b, lc, d), final_norm_g, s_rows)
```

```python
import functools

import jax
import jax.numpy as jnp
from jax import lax
from jax.experimental import pallas as pl
from jax.experimental.pallas import tpu as pltpu

N_HEADS = 16
QK_NOPE = 128
QK_ROPE = 64
QK_HEAD = QK_NOPE + QK_ROPE
V_HEAD = 128
GRID_W = 64
ROPE_THETA = 10000.0
RG_BLOCKS = 8
CONV_W = 4
CONV_PAD_L = 2
RG_C = 8.0
N_GROUPS = 4
EPS = 1e-6
N_MOD = 6

LANES = 128
SUBLANES = 8
ROW_BLOCK = 256
KEY_CHUNK = 256
DMA_UNROLL = 8
SCAN_PAD = 4
LOG2E = 1.4426950408889634
VMEM_LIMIT = 56 * 1024 * 1024

F32 = jnp.float32
BF16 = jnp.bfloat16


def _pick(n, candidates):
    for c in candidates:
        if n % c == 0:
            return c
    raise ValueError(f"no tile of {candidates} divides {n}")


def _params(*sem):
    return pltpu.CompilerParams(dimension_semantics=sem, vmem_limit_bytes=VMEM_LIMIT)


def _mod_row(blk, nt, nctx, nb):
    return jnp.where(blk % nt >= nt - nctx, nb, blk // nt)


def _rms(x):
    return x * lax.rsqrt(jnp.mean(x * x, axis=-1, keepdims=True) + EPS)


def _mod_kernel(c_ref, w_ref, b_ref, o_ref):
    c = c_ref[...]
    s = (c * jax.nn.sigmoid(c)).astype(BF16)
    o_ref[...] = jnp.dot(s, w_ref[...].astype(BF16), preferred_element_type=F32) + b_ref[...]


def _mod_table(cc, w_mod, b_mod):
    depth, d, n = w_mod.shape
    tn = _pick(n, (1024, 512, 256, 128))
    return pl.pallas_call(
        _mod_kernel,
        grid=(depth, n // tn),
        in_specs=[
            pl.BlockSpec((16, d), lambda l, j: (0, 0)),
            pl.BlockSpec((None, d, tn), lambda l, j: (l, 0, j)),
            pl.BlockSpec((None, 1, tn), lambda l, j: (l, 0, j)),
        ],
        out_specs=pl.BlockSpec((None, 16, tn), lambda l, j: (l, 0, j)),
        out_shape=jax.ShapeDtypeStruct((depth, 16, n), F32),
        compiler_params=_params("arbitrary", "arbitrary"),
        name="mod_table",
    )(cc, w_mod, b_mod.reshape(depth, 1, n))


def _inproj_kernel(x_ref, g_ref, sh_ref, sc_ref, w_ref, o_ref, h_ref, *, tm, nt, nctx, nb):
    i = pl.program_id(0)

    @pl.when(pl.program_id(1) == 0)
    def _():
        for s in range(tm // ROW_BLOCK):
            rows = slice(s * ROW_BLOCK, (s + 1) * ROW_BLOCK)
            row = _mod_row(i * (tm // ROW_BLOCK) + s, nt, nctx, nb)
            y = _rms(x_ref[rows, :]) * g_ref[...]
            y = y * (1.0 + sc_ref[pl.ds(row, 1), :]) + sh_ref[pl.ds(row, 1), :]
            h_ref[rows, :] = y.astype(BF16)

    o_ref[...] = jnp.dot(h_ref[...], w_ref[...], preferred_element_type=F32).astype(BF16)


def _inproj(x2, norm_g, mod_t, w_in_p, l, geo):
    m, d = x2.shape
    n = w_in_p.shape[-1]
    tm = _pick(m, (1024, 512, 256))
    tn = _pick(n, (1024, 512, 256, 128))
    kern = functools.partial(_inproj_kernel, tm=tm, nt=geo["nt"], nctx=geo["nctx"], nb=geo["nb"])
    return pl.pallas_call(
        kern,
        grid=(m // tm, n // tn),
        in_specs=[
            pl.BlockSpec((tm, d), lambda i, j: (i, 0)),
            pl.BlockSpec((None, 1, d), lambda i, j: (l, 0, 0)),
            pl.BlockSpec((None, None, 16, d), lambda i, j: (l, 0, 0, 0)),
            pl.BlockSpec((None, None, 16, d), lambda i, j: (l, 1, 0, 0)),
            pl.BlockSpec((None, d, tn), lambda i, j: (l, 0, j)),
        ],
        out_specs=pl.BlockSpec((tm, tn), lambda i, j: (i, j)),
        out_shape=jax.ShapeDtypeStruct((m, n), BF16),
        scratch_shapes=[pltpu.VMEM((tm, d), BF16)],
        compiler_params=_params("arbitrary", "arbitrary"),
        name="inproj",
    )(x2, norm_g, mod_t, mod_t, w_in_p)


def _kvup_kernel(c_ref, g_ref, w_ref, o_ref, h_ref):
    @pl.when(pl.program_id(1) == 0)
    def _():
        h_ref[...] = (_rms(c_ref[...].astype(F32)) * g_ref[...]).astype(BF16)

    o_ref[...] = jnp.dot(h_ref[...], w_ref[...], preferred_element_type=F32).astype(BF16)


def _kvup(p, kv_g, w_ukv_p, l, off_ckv):
    m = p.shape[0]
    r, n = w_ukv_p.shape[1:]
    tm = _pick(m, (1024, 512, 256))
    tn = _pick(n, (2048, 1024, 512, 256))
    cb = off_ckv // r
    return pl.pallas_call(
        _kvup_kernel,
        grid=(m // tm, n // tn),
        in_specs=[
            pl.BlockSpec((tm, r), lambda i, j: (i, cb)),
            pl.BlockSpec((None, 1, r), lambda i, j: (l, 0, 0)),
            pl.BlockSpec((None, r, tn), lambda i, j: (l, 0, j)),
        ],
        out_specs=pl.BlockSpec((tm, tn), lambda i, j: (i, j)),
        out_shape=jax.ShapeDtypeStruct((m, n), BF16),
        scratch_shapes=[pltpu.VMEM((tm, r), BF16)],
        compiler_params=_params("arbitrary", "arbitrary"),
        name="kvup",
    )(p, kv_g, w_ukv_p)


def _rope(x, cos, sin_signed):
    lane = lax.broadcasted_iota(jnp.int32, x.shape, 1)
    first = (lane % 32) < 16
    rot = jnp.where(first, pltpu.roll(x, LANES - 16, 1), pltpu.roll(x, 16, 1))
    return x * cos + rot * sin_signed


def _key_chunks(n):
    return [(k0, min(KEY_CHUNK, n - k0)) for k0 in range(0, n, KEY_CHUNK)]


def _attn_kernel(qn_ref, qr_ref, kn_ref, kr_ref, v_ref, cos_ref, sin_ref, o_in_ref, o_ref, kcat_ref,
                 *, tq, lc):
    del o_in_ref
    qi = pl.program_id(2)

    @pl.when(qi == 0)
    def _():
        kr = _rope(kr_ref[...].astype(F32), cos_ref[...], sin_ref[...])
        for j in range(2):
            kcat_ref[j, :, 0:QK_NOPE] = kn_ref[:, j * QK_NOPE:(j + 1) * QK_NOPE]
            krj = kr if j == 0 else pltpu.roll(kr, QK_ROPE, 1)
            kcat_ref[j, :, QK_NOPE:QK_NOPE + LANES] = krj.astype(BF16)

    scale = QK_HEAD ** -0.5 * LOG2E
    r0 = pl.multiple_of(qi * tq, tq)
    qr = _rope(qr_ref[...].astype(F32), cos_ref[pl.ds(r0, tq), :], sin_ref[pl.ds(r0, tq), :])
    qr = (qr * scale).astype(BF16)
    qn = (qn_ref[...].astype(F32) * scale).astype(BF16)
    nt_dims = (((1,), (1,)), ((), ()))

    for j in range(2):
        q = jnp.concatenate([qn[:, j * QK_NOPE:(j + 1) * QK_NOPE], qr], axis=1)
        m = part = acc = None
        for k0, kn in _key_chunks(lc):
            s = lax.dot_general(q, kcat_ref[j, k0:k0 + kn, :], nt_dims, preferred_element_type=F32)
            smax = jnp.max(s, axis=-1, keepdims=True)
            m_new = smax if m is None else jnp.maximum(m, smax)
            p = jnp.exp2(s - m_new)
            psum = p[:, 0:LANES]
            for c in range(1, kn // LANES):
                psum = psum + p[:, c * LANES:(c + 1) * LANES]
            pv = jnp.dot(p.astype(BF16), v_ref[k0:k0 + kn, j * V_HEAD:(j + 1) * V_HEAD],
                         preferred_element_type=F32)
            if m is None:
                part, acc = psum, pv
            else:
                alpha = jnp.exp2(m - m_new)
                part = alpha * part + psum
                acc = alpha * acc + pv
            m = m_new
        den = jnp.sum(part, axis=-1, keepdims=True)
        o_ref[:, j * V_HEAD:(j + 1) * V_HEAD] = (acc / den).astype(BF16)


def _attn_ctx_kernel(qn_ref, qr_ref, kn_ref, kr_ref, v_ref, o_in_ref, o_ref):
    del o_in_ref
    scale = QK_HEAD ** -0.5
    qr = (qr_ref[...].astype(F32) * scale).astype(BF16)
    qn = (qn_ref[...].astype(F32) * scale).astype(BF16)
    kr = kr_ref[...].astype(F32)
    for j in range(2):
        krj = kr if j == 0 else pltpu.roll(kr, QK_ROPE, 1)
        k = jnp.concatenate([kn_ref[:, j * QK_NOPE:(j + 1) * QK_NOPE], krj.astype(BF16)], axis=1)
        q = jnp.concatenate([qn[:, j * QK_NOPE:(j + 1) * QK_NOPE], qr], axis=1)
        s = lax.dot_general(q, k, (((1,), (1,)), ((), ())), preferred_element_type=F32)
        p = jnp.exp(s - jnp.max(s, axis=-1, keepdims=True))
        den = jnp.sum(p, axis=-1, keepdims=True)
        o = jnp.dot(p.astype(BF16), v_ref[:, j * V_HEAD:(j + 1) * V_HEAD], preferred_element_type=F32)
        o_ref[:, j * V_HEAD:(j + 1) * V_HEAD] = (o / den).astype(BF16)


def _attention(p3, kv3, cos_t, sin_t, s_rows, off_qr, off_kr):
    b, lc, _ = p3.shape
    c_rows = lc - s_rows
    hp = N_HEADS // 2
    tq = _pick(s_rows, (1024, 512, 256))
    qrb = off_qr // LANES
    krb = off_kr // LANES
    o = pl.pallas_call(
        functools.partial(_attn_kernel, tq=tq, lc=lc),
        grid=(b, hp, s_rows // tq),
        in_specs=[
            pl.BlockSpec((None, tq, 2 * QK_NOPE), lambda bi, h, q: (bi, q, h)),
            pl.BlockSpec((None, tq, LANES), lambda bi, h, q: (bi, q, qrb + h)),
            pl.BlockSpec((None, lc, 2 * QK_NOPE), lambda bi, h, q: (bi, 0, h)),
            pl.BlockSpec((None, lc, LANES), lambda bi, h, q: (bi, 0, krb)),
            pl.BlockSpec((None, lc, 2 * V_HEAD), lambda bi, h, q: (bi, 0, hp + h)),
            pl.BlockSpec((lc, LANES), lambda bi, h, q: (0, 0)),
            pl.BlockSpec((lc, LANES), lambda bi, h, q: (0, 0)),
            pl.BlockSpec(memory_space=pl.ANY),
        ],
        out_specs=pl.BlockSpec((None, tq, 2 * V_HEAD), lambda bi, h, q: (bi, q, h)),
        out_shape=jax.ShapeDtypeStruct((b, lc, N_HEADS * V_HEAD), BF16),
        scratch_shapes=[pltpu.VMEM((2, lc, QK_NOPE + LANES), BF16)],
        input_output_aliases={7: 0},
        compiler_params=_params("arbitrary", "arbitrary", "arbitrary"),
        name="attention",
    )(p3, p3, kv3, p3, kv3, cos_t, sin_t, jnp.zeros((b, lc, N_HEADS * V_HEAD), BF16))
    cb = s_rows // c_rows
    return pl.pallas_call(
        _attn_ctx_kernel,
        grid=(b, hp),
        in_specs=[
            pl.BlockSpec((None, c_rows, 2 * QK_NOPE), lambda bi, h: (bi, cb, h)),
            pl.BlockSpec((None, c_rows, LANES), lambda bi, h: (bi, cb, qrb + h)),
            pl.BlockSpec((None, c_rows, 2 * QK_NOPE), lambda bi, h: (bi, cb, h)),
            pl.BlockSpec((None, c_rows, LANES), lambda bi, h: (bi, cb, krb)),
            pl.BlockSpec((None, c_rows, 2 * V_HEAD), lambda bi, h: (bi, cb, hp + h)),
            pl.BlockSpec(memory_space=pl.ANY),
        ],
        out_specs=pl.BlockSpec((None, c_rows, 2 * V_HEAD), lambda bi, h: (bi, cb, h)),
        out_shape=jax.ShapeDtypeStruct(o.shape, BF16),
        input_output_aliases={5: 0},
        compiler_params=_params("arbitrary", "arbitrary"),
        name="attention_ctx",
    )(p3, p3, kv3, p3, kv3, o)


def _scan_tile(a_ref, b_ref, h_ref, o_ref, *, seg, pitch, reverse):
    nslab = a_ref.shape[0]

    def body(g, carry):
        row = (seg - 1 - g) if reverse else g
        idx = pl.ds(row, SUBLANES, stride=pitch)
        out = []
        for c in range(nslab):
            av = a_ref[c, idx, :]
            hc = av * carry[2 * c] + b_ref[c, idx, :]
            ac = av * carry[2 * c + 1]
            b_ref[c, idx, :] = hc
            a_ref[c, idx, :] = ac
            out += [hc, ac]
        return tuple(out)

    init = (jnp.zeros((SUBLANES, LANES), F32), jnp.ones((SUBLANES, LANES), F32)) * nslab
    fin = lax.fori_loop(0, seg, body, init)
    sub = lax.broadcasted_iota(jnp.int32, (SUBLANES, LANES), 0)
    for c in range(nslab):
        lanes = slice(c * LANES, (c + 1) * LANES)
        h_end, a_end = fin[2 * c], fin[2 * c + 1]
        cin = h_ref[:, lanes]
        for s in (range(SUBLANES - 1, 0, -1) if reverse else range(SUBLANES - 1)):
            nxt = a_end * cin + h_end
            if reverse:
                cin = jnp.where(sub == s - 1, pltpu.roll(nxt, SUBLANES - 1, 0), cin)
            else:
                cin = jnp.where(sub == s + 1, pltpu.roll(nxt, 1, 0), cin)
        out = a_end * cin + h_end
        edge = out[0:1, :] if reverse else out[SUBLANES - 1:SUBLANES, :]
        h_ref[:, lanes] = jnp.broadcast_to(edge, (SUBLANES, LANES))
        for s in range(SUBLANES):
            rows = slice(s * pitch, s * pitch + seg)
            h = b_ref[c, rows, :] + a_ref[c, rows, :] * cin[s:s + 1, :]
            o_ref[s * seg:(s + 1) * seg, lanes] = h.astype(BF16)


def _rglru_kernel(x_ref, xp_ref, xn_ref, cw_ref, cb_ref, w_ref, rp_ref, o_ref, a_ref, b_ref, h_ref,
                  *, tt, nt, nctx, d, bw):
    dr = pl.program_id(0)
    i = pl.program_id(2)
    t = _tile_of(dr, i, nt, nctx)
    nlat = nt - nctx
    seg = tt // SUBLANES
    pitch = a_ref.shape[1] // SUBLANES
    first = jnp.logical_or(t == 0, t == nlat)
    last = jnp.logical_or(t == nlat - 1, t == nt - 1)

    @pl.when(i == 0)
    def _():
        h_ref[...] = jnp.zeros_like(h_ref)

    x = x_ref[...]
    rr = lax.broadcasted_iota(jnp.int32, (tt, tt), 0)
    cc = lax.broadcasted_iota(jnp.int32, (tt, tt), 1)
    offsets = [k - CONV_PAD_L for k in range(CONV_W) if k != CONV_PAD_L]
    shift = jnp.concatenate([(cc == rr + o).astype(BF16) for o in offsets], axis=0)
    taps = jnp.dot(shift, x, preferred_element_type=F32)
    xr = cb_ref[...] + x.astype(F32) * cw_ref[CONV_PAD_L:CONV_PAD_L + 1, :]
    for j, o in enumerate(offsets):
        xr = xr + taps[j * tt:(j + 1) * tt, :] * cw_ref[o + CONV_PAD_L:o + CONV_PAD_L + 1, :]
    prev = jnp.where(first, 0.0, xp_ref[...].astype(F32)[SUBLANES:, :])
    nxt = jnp.where(last, 0.0, xn_ref[...].astype(F32)[:SUBLANES, :])
    sub = lax.broadcasted_iota(jnp.int32, prev.shape, 0)
    head = jnp.zeros_like(prev)
    tail = jnp.zeros_like(prev)
    for o in offsets:
        w = cw_ref[o + CONV_PAD_L:o + CONV_PAD_L + 1, :]
        if o < 0:
            head = head + jnp.where(sub < -o, pltpu.roll(prev, -o, 0), 0.0) * w
        else:
            tail = tail + jnp.where(sub >= SUBLANES - o, pltpu.roll(nxt, SUBLANES - o, 0), 0.0) * w
    xr = jnp.concatenate([xr[0:SUBLANES] + head, xr[SUBLANES:tt - SUBLANES], xr[tt - SUBLANES:] + tail], axis=0)
    xb = xr.astype(BF16)

    hba = 0.5 * rp_ref[0:1, :]
    hbi = 0.5 * rp_ref[1:2, :]
    ca = (-0.5 * RG_C * LOG2E) * jax.nn.softplus(-rp_ref[2:3, :])
    hx = 0.5 * xr
    for n in range(d // bw):
        cols = slice(n * bw, (n + 1) * bw)
        g = jnp.dot(xb[:, cols], w_ref[n], preferred_element_type=F32)
        ta = jnp.tanh(g[:, :bw] + hba[:, cols])
        ti = jnp.tanh(g[:, bw:] + hbi[:, cols])
        a = jnp.exp2(ca[:, cols] * ta + ca[:, cols])
        bx = jnp.sqrt(1.0 - a * a) * ((ti + 1.0) * hx[:, cols])
        for c in range(bw // LANES):
            slab = n * (bw // LANES) + c
            lanes = slice(c * LANES, (c + 1) * LANES)
            for s in range(SUBLANES):
                a_ref[slab, s * pitch:s * pitch + seg, :] = a[s * seg:(s + 1) * seg, lanes]
                b_ref[slab, s * pitch:s * pitch + seg, :] = bx[s * seg:(s + 1) * seg, lanes]

    @pl.when(dr == 0)
    def _():
        _scan_tile(a_ref, b_ref, h_ref, o_ref, seg=seg, pitch=pitch, reverse=False)

    @pl.when(dr == 1)
    def _():
        _scan_tile(a_ref, b_ref, h_ref, o_ref, seg=seg, pitch=pitch, reverse=True)


def _tile_of(dr, i, nt, nctx):
    nlat = nt - nctx
    fwd = jnp.where(i < nctx, nlat + i, i - nctx)
    back = jnp.where(i < nctx, nt - 1 - i, nlat - 1 - (i - nctx))
    return jnp.where(dr == 0, fwd, back)


def _rglru(p3, conv_w, conv_b, rg_w, rg_p, l, geo, off_rx):
    b, lc, _ = p3.shape
    d = conv_w.shape[-1]
    bw = d // RG_BLOCKS
    tt = ROW_BLOCK
    nt = lc // tt
    nctx = geo["nctx"]
    xb = off_rx // d
    hb = tt // 16
    nh = lc // 16

    def tile(dr, i):
        return _tile_of(dr, i, nt, nctx)

    kern = functools.partial(_rglru_kernel, tt=tt, nt=nt, nctx=nctx, d=d, bw=bw)
    return pl.pallas_call(
        kern,
        grid=(2, b, nt),
        in_specs=[
            pl.BlockSpec((None, tt, d), lambda dr, bi, i: (bi, tile(dr, i), xb)),
            pl.BlockSpec((None, 16, d), lambda dr, bi, i: (bi, jnp.maximum(tile(dr, i) * hb - 1, 0), xb)),
            pl.BlockSpec((None, 16, d), lambda dr, bi, i: (bi, jnp.minimum((tile(dr, i) + 1) * hb, nh - 1), xb)),
            pl.BlockSpec((None, CONV_W, d), lambda dr, bi, i: (l, 0, 0)),
            pl.BlockSpec((None, 1, d), lambda dr, bi, i: (l, 0, 0)),
            pl.BlockSpec((None, None, RG_BLOCKS, bw, 2 * bw), lambda dr, bi, i: (l, dr, 0, 0, 0)),
            pl.BlockSpec((None, None, 3, d), lambda dr, bi, i: (l, dr, 0, 0)),
        ],
        out_specs=pl.BlockSpec((None, None, tt, d), lambda dr, bi, i: (dr, bi, tile(dr, i), 0)),
        out_shape=jax.ShapeDtypeStruct((2, b, lc, d), BF16),
        scratch_shapes=[
            pltpu.VMEM((d // LANES, tt + SUBLANES * SCAN_PAD, LANES), F32),
            pltpu.VMEM((d // LANES, tt + SUBLANES * SCAN_PAD, LANES), F32),
            pltpu.VMEM((SUBLANES, d), F32),
        ],
        compiler_params=_params("arbitrary", "arbitrary", "arbitrary"),
        name="rglru",
    )(p3, p3, p3, conv_w, conv_b, rg_w, rg_p)


def _merge_kernel(o_ref, hf_ref, hb_ref, ry_ref, gm_ref, gr_ref, wm_ref, wr_ref, z_ref, r_ref):
    @pl.when(pl.program_id(1) == 0)
    def _():
        h = hf_ref[...].astype(F32) + hb_ref[...].astype(F32)
        r_ref[...] = (h * jax.nn.gelu(ry_ref[...].astype(F32))).astype(BF16)

    ym = jnp.dot(o_ref[...], wm_ref[...], preferred_element_type=F32)
    yr = jnp.dot(r_ref[...], wr_ref[...], preferred_element_type=F32)
    z = jax.nn.sigmoid(gm_ref[...].astype(F32)) * ym + jax.nn.sigmoid(gr_ref[...].astype(F32)) * yr
    z_ref[...] = z.astype(BF16)


def _merge(o2, h2d, p, w_o_mla, w_o_rnn, l, off_ry, off_gm, off_gr):
    m, d = o2.shape
    tm = ROW_BLOCK
    tn = d
    ryb = off_ry // d
    gmb = off_gm // tn
    grb = off_gr // tn
    return pl.pallas_call(
        _merge_kernel,
        grid=(m // tm, d // tn),
        in_specs=[
            pl.BlockSpec((tm, d), lambda i, j: (i, 0)),
            pl.BlockSpec((None, tm, d), lambda i, j: (0, i, 0)),
            pl.BlockSpec((None, tm, d), lambda i, j: (1, i, 0)),
            pl.BlockSpec((tm, d), lambda i, j: (i, ryb)),
            pl.BlockSpec((tm, tn), lambda i, j: (i, gmb + j)),
            pl.BlockSpec((tm, tn), lambda i, j: (i, grb + j)),
            pl.BlockSpec((None, d, tn), lambda i, j: (l, 0, j)),
            pl.BlockSpec((None, d, tn), lambda i, j: (l, 0, j)),
        ],
        out_specs=pl.BlockSpec((tm, tn), lambda i, j: (i, j)),
        out_shape=jax.ShapeDtypeStruct((m, d), BF16),
        scratch_shapes=[pltpu.VMEM((tm, d), BF16)],
        compiler_params=_params("arbitrary", "arbitrary"),
        name="merge",
    )(o2, h2d, h2d, p, p, p, w_o_mla, w_o_rnn)


def _outproj_kernel(z_ref, w_ref, x_ref, g_ref, o_ref, *, tm, nt, nctx, nb):
    i = pl.program_id(0)
    y = jnp.dot(z_ref[...], w_ref[...], preferred_element_type=F32)
    for s in range(tm // ROW_BLOCK):
        rows = slice(s * ROW_BLOCK, (s + 1) * ROW_BLOCK)
        row = _mod_row(i * (tm // ROW_BLOCK) + s, nt, nctx, nb)
        o_ref[rows, :] = x_ref[rows, :] + g_ref[pl.ds(row, 1), :] * y[rows, :]


def _outproj(z, w_out, x2, mod_t, l, geo):
    m, d = x2.shape
    tm = _pick(m, (1024, 512, 256))
    tn = _pick(d, (512, 256))
    kern = functools.partial(_outproj_kernel, tm=tm, nt=geo["nt"], nctx=geo["nctx"], nb=geo["nb"])
    return pl.pallas_call(
        kern,
        grid=(m // tm, d // tn),
        in_specs=[
            pl.BlockSpec((tm, d), lambda i, j: (i, 0)),
            pl.BlockSpec((None, d, tn), lambda i, j: (l, 0, j)),
            pl.BlockSpec((tm, tn), lambda i, j: (i, j)),
            pl.BlockSpec((None, None, 16, tn), lambda i, j: (l, 2, 0, j)),
        ],
        out_specs=pl.BlockSpec((tm, tn), lambda i, j: (i, j)),
        out_shape=jax.ShapeDtypeStruct((m, d), F32),
        input_output_aliases={2: 0},
        compiler_params=_params("arbitrary", "arbitrary"),
        name="outproj",
    )(z, w_out, x2, mod_t)


def _pairs(per):
    return [(a, c) for a in range(per) for c in range(a + 1, per)]


def _router(h, wr_hi, wr_lo, rb):
    hi = h.astype(BF16)
    lo = (h - hi.astype(F32)).astype(BF16)
    nt_dims = (((1,), (1,)), ((), ()))
    logits = (lax.dot_general(wr_hi, hi, nt_dims, preferred_element_type=F32)
              + lax.dot_general(wr_hi, lo, nt_dims, preferred_element_type=F32)
              + lax.dot_general(wr_lo, hi, nt_dims, preferred_element_type=F32))
    scores = jax.nn.sigmoid(logits)
    biased = scores + rb
    e = scores.shape[0]
    per = e // N_GROUPS
    rows_b = [biased[j:j + 1, :] for j in range(e)]
    rows_s = [scores[j:j + 1, :] for j in range(e)]
    gscore = []
    for g in range(N_GROUPS):
        r = rows_b[g * per:(g + 1) * per]
        best = None
        for a in range(per):
            for c in range(a + 1, per):
                pair = r[a] + r[c]
                best = pair if best is None else jnp.maximum(best, pair)
        gscore.append(best)
    gbest = gscore[0]
    gidx = jnp.zeros_like(gbest, dtype=jnp.int32)
    for g in range(1, N_GROUPS):
        better = gscore[g] > gbest
        gbest = jnp.where(better, gscore[g], gbest)
        gidx = jnp.where(better, g, gidx)
    sel = []
    for g in range(N_GROUPS):
        r = rows_b[g * per:(g + 1) * per]
        for a in range(per):
            rank = jnp.zeros_like(gidx)
            for c in range(per):
                if c == a:
                    continue
                ahead = (r[c] > r[a]) if c > a else (r[c] >= r[a])
                rank = rank + ahead.astype(jnp.int32)
            sel.append(jnp.logical_and(gidx == g, rank < 2))
    den = None
    for j in range(e):
        term = jnp.where(sel[j], rows_s[j], 0.0)
        den = term if den is None else den + term
    cls = jnp.zeros_like(gidx)
    w_lo = jnp.zeros_like(den)
    w_hi = jnp.zeros_like(den)
    pairs = _pairs(per)
    for g in range(N_GROUPS):
        for pi, (a, c) in enumerate(pairs):
            both = jnp.logical_and(sel[g * per + a], sel[g * per + c])
            cls = jnp.where(both, g * len(pairs) + pi, cls)
            w_lo = jnp.where(both, rows_s[g * per + a] / den, w_lo)
            w_hi = jnp.where(both, rows_s[g * per + c] / den, w_hi)
    return cls, w_lo, w_hi


def _ffnprep_kernel(x_ref, g_ref, sh_ref, sc_ref, wh_ref, wl_ref, rb_ref, hs_ref, meta_ref, cnt_ref, run_ref,
                    *, tm, nt, nctx, nb, d):
    i = pl.program_id(0)

    @pl.when(i == 0)
    def _():
        run_ref[...] = jnp.zeros_like(run_ref)

    hs = []
    for s in range(tm // ROW_BLOCK):
        rows = slice(s * ROW_BLOCK, (s + 1) * ROW_BLOCK)
        row = _mod_row(i * (tm // ROW_BLOCK) + s, nt, nctx, nb)
        y = _rms(x_ref[rows, :]) * g_ref[...]
        y = y * (1.0 + sc_ref[pl.ds(row, 1), :]) + sh_ref[pl.ds(row, 1), :]
        hs_ref[rows, 0:d] = y
        hs.append(y)
    h = jnp.concatenate(hs, axis=0) if len(hs) > 1 else hs[0]
    cls, w_lo, w_hi = _router(h, wh_ref[...], wl_ref[...], rb_ref[...])
    extra = jnp.concatenate([w_lo, w_hi, jnp.zeros((LANES - 2, tm), F32)], axis=0)
    hs_ref[:, d:d + LANES] = jnp.transpose(extra)

    nc = run_ref.shape[0]
    onehot = (lax.broadcasted_iota(jnp.int32, (nc, tm), 0) == cls).astype(F32)
    before = lax.broadcasted_iota(jnp.int32, (tm, tm), 0) < lax.broadcasted_iota(jnp.int32, (tm, tm), 1)
    prefix = jnp.dot(onehot.astype(BF16), before.astype(BF16), preferred_element_type=F32)
    run = run_ref[:, 0:1]
    rank = jnp.sum(onehot * (prefix + run), axis=0, keepdims=True).astype(jnp.int32)
    meta_ref[...] = jnp.concatenate([cls, rank, jnp.zeros((SUBLANES - 2, tm), jnp.int32)], axis=0)
    run_ref[...] = run_ref[...] + jnp.sum(onehot, axis=1, keepdims=True)
    cnt_ref[...] = run_ref[...]


def _ffnprep(x2, norm_g, mod_t, wr_hi, wr_lo, rbias, l, geo, ncls):
    m, d = x2.shape
    e = wr_hi.shape[0]
    tm = _pick(m, (512, 256))
    nc = -(-ncls // SUBLANES) * SUBLANES
    kern = functools.partial(_ffnprep_kernel, tm=tm, nt=geo["nt"], nctx=geo["nctx"], nb=geo["nb"], d=d)
    return pl.pallas_call(
        kern,
        grid=(m // tm,),
        in_specs=[
            pl.BlockSpec((tm, d), lambda i: (i, 0)),
            pl.BlockSpec((None, 1, d), lambda i: (l, 0, 0)),
            pl.BlockSpec((None, None, 16, d), lambda i: (l, 3, 0, 0)),
            pl.BlockSpec((None, None, 16, d), lambda i: (l, 4, 0, 0)),
            pl.BlockSpec((e, d), lambda i: (0, 0)),
            pl.BlockSpec((e, d), lambda i: (0, 0)),
            pl.BlockSpec((e, 1), lambda i: (0, 0)),
        ],
        out_specs=[
            pl.BlockSpec((tm, d + LANES), lambda i: (i, 0)),
            pl.BlockSpec((SUBLANES, tm), lambda i: (0, i)),
            pl.BlockSpec((nc, LANES), lambda i: (0, 0)),
        ],
        out_shape=[jax.ShapeDtypeStruct((m, d + LANES), F32), jax.ShapeDtypeStruct((SUBLANES, m), jnp.int32),
                   jax.ShapeDtypeStruct((nc, LANES), F32)],
        scratch_shapes=[pltpu.VMEM((nc, LANES), F32)],
        compiler_params=_params("arbitrary"),
        name="ffnprep",
    )(x2, norm_g, mod_t, mod_t, wr_hi, wr_lo, rbias)


def _row_copy(src_ref, src_row, dst_ref, dst_row, sem):
    return pltpu.make_async_copy(src_ref.at[pl.ds(src_row, 1)], dst_ref.at[pl.ds(dst_row, 1)], sem)


def _dispatch_kernel(pos_ref, hs_ref, xs_in_ref, xs_ref, sem, *, tm):
    del xs_in_ref
    base = pl.program_id(0) * tm

    def start(r, carry):
        _row_copy(hs_ref, r, xs_ref, pos_ref[base + r], sem).start()
        return carry

    def wait(r, carry):
        _row_copy(hs_ref, r, xs_ref, pos_ref[base + r], sem).wait()
        return carry

    lax.fori_loop(0, tm, start, 0, unroll=DMA_UNROLL)
    lax.fori_loop(0, tm, wait, 0, unroll=DMA_UNROLL)


def _dispatch(pos, hs, xs0):
    m, w = hs.shape
    tm = _pick(m, (512, 256))
    return pl.pallas_call(
        functools.partial(_dispatch_kernel, tm=tm),
        grid_spec=pltpu.PrefetchScalarGridSpec(
            num_scalar_prefetch=1,
            grid=(m // tm,),
            in_specs=[
                pl.BlockSpec((tm, w), lambda i, pos: (i, 0)),
                pl.BlockSpec(memory_space=pl.ANY),
            ],
            out_specs=pl.BlockSpec(memory_space=pl.ANY),
            scratch_shapes=[pltpu.SemaphoreType.DMA(())],
        ),
        out_shape=jax.ShapeDtypeStruct(xs0.shape, F32),
        input_output_aliases={2: 0},
        compiler_params=_params("arbitrary"),
        name="dispatch",
    )(pos, hs, xs0)


def _expert_kernel(eid_ref, used_ref, xs_ref, wg_ref, wu_ref, wd_ref, *rest, d, k):
    del eid_ref
    o_ref = rest[-1]
    t = pl.program_id(0)

    @pl.when(t < used_ref[0])
    def _():
        x = xs_ref[:, 0:d].astype(BF16)
        gate = jnp.dot(x, wg_ref[...], preferred_element_type=F32)
        up = jnp.dot(x, wu_ref[...], preferred_element_type=F32)
        act = (gate * jax.nn.sigmoid(gate) * up).astype(BF16)
        y = jnp.dot(act, wd_ref[...], preferred_element_type=F32)
        y = xs_ref[:, d + k:d + k + 1] * y
        o_ref[...] = y if k == 0 else rest[0][...] + y

    @pl.when(t >= used_ref[0])
    def _():
        o_ref[...] = jnp.zeros_like(o_ref)


def _experts(eid, used, xs, w_gate, w_up, w_down, prev, l, tm, k):
    p, w = xs.shape
    d = w - LANES
    f = w_gate.shape[-1]
    nt = p // tm
    row_spec = pl.BlockSpec((tm, d), lambda t, eid, used: (t, 0))
    return pl.pallas_call(
        functools.partial(_expert_kernel, d=d, k=k),
        grid_spec=pltpu.PrefetchScalarGridSpec(
            num_scalar_prefetch=2,
            grid=(nt,),
            in_specs=[
                pl.BlockSpec((tm, w), lambda t, eid, used: (t, 0)),
                pl.BlockSpec((None, None, d, f), lambda t, eid, used: (l, eid[k * nt + t], 0, 0)),
                pl.BlockSpec((None, None, d, f), lambda t, eid, used: (l, eid[k * nt + t], 0, 0)),
                pl.BlockSpec((None, None, f, d), lambda t, eid, used: (l, eid[k * nt + t], 0, 0)),
            ] + [row_spec] * len(prev),
            out_specs=row_spec,
        ),
        out_shape=jax.ShapeDtypeStruct((p, d), F32),
        compiler_params=_params("arbitrary"),
        name="experts",
    )(eid, used, xs, w_gate, w_up, w_down, *prev)


def _combine_kernel(pos_ref, x_ref, ys_ref, g_ref, o_ref, y_ref, sem, *, tm, nt, nctx, nb):
    i = pl.program_id(0)
    base = i * tm

    def start(r, carry):
        _row_copy(ys_ref, pos_ref[base + r], y_ref, r, sem).start()
        return carry

    def wait(r, carry):
        _row_copy(ys_ref, pos_ref[base + r], y_ref, r, sem).wait()
        return carry

    lax.fori_loop(0, tm, start, 0, unroll=DMA_UNROLL)
    lax.fori_loop(0, tm, wait, 0, unroll=DMA_UNROLL)
    for s in range(tm // ROW_BLOCK):
        rows = slice(s * ROW_BLOCK, (s + 1) * ROW_BLOCK)
        row = _mod_row(i * (tm // ROW_BLOCK) + s, nt, nctx, nb)
        o_ref[rows, :] = x_ref[rows, :] + g_ref[pl.ds(row, 1), :] * y_ref[rows, :]


def _combine(pos, x2, ys, mod_t, l, geo):
    m, d = x2.shape
    tm = _pick(m, (512, 256))
    kern = functools.partial(_combine_kernel, tm=tm, nt=geo["nt"], nctx=geo["nctx"], nb=geo["nb"])
    return pl.pallas_call(
        kern,
        grid_spec=pltpu.PrefetchScalarGridSpec(
            num_scalar_prefetch=1,
            grid=(m // tm,),
            in_specs=[
                pl.BlockSpec((tm, d), lambda i, pos: (i, 0)),
                pl.BlockSpec(memory_space=pl.ANY),
                pl.BlockSpec((None, None, 16, d), lambda i, pos: (l, 5, 0, 0)),
            ],
            out_specs=pl.BlockSpec((tm, d), lambda i, pos: (i, 0)),
            scratch_shapes=[pltpu.VMEM((tm, d), F32), pltpu.SemaphoreType.DMA(())],
        ),
        out_shape=jax.ShapeDtypeStruct((m, d), F32),
        input_output_aliases={1: 0},
        compiler_params=_params("arbitrary"),
        name="combine",
    )(pos, x2, ys, mod_t)


def _moe_ffn(x2, norm_g, mod_t, wr_hi, wr_lo, rbias, w_gate, w_up, w_down, l, geo):
    m, d = x2.shape
    ne = w_gate.shape[1]
    per = ne // N_GROUPS
    pairs = _pairs(per)
    ncls = N_GROUPS * len(pairs)
    tm = ROW_BLOCK
    nt = m // tm + ncls
    hs, meta, cnt = _ffnprep(x2, norm_g, mod_t, wr_hi, wr_lo, rbias, l, geo, ncls)

    count = cnt[:ncls, 0].astype(jnp.int32)
    tiles = (count + tm - 1) // tm
    tile_end = jnp.cumsum(tiles)
    row0 = (tile_end - tiles) * tm
    pos = row0[meta[0]] + meta[1]
    used = tile_end[-1]
    tidx = jnp.minimum(jnp.arange(nt, dtype=jnp.int32), used - 1)
    tile_cls = jnp.sum((tile_end[None, :] <= tidx[:, None]).astype(jnp.int32), axis=1)
    e_lo = jnp.array([g * per + a for g in range(N_GROUPS) for a, _ in pairs], jnp.int32)
    e_hi = jnp.array([g * per + c for g in range(N_GROUPS) for _, c in pairs], jnp.int32)
    eid = jnp.concatenate([e_lo[tile_cls], e_hi[tile_cls]])

    xs = _dispatch(pos, hs, jnp.zeros((nt * tm, d + LANES), F32))
    used = used.reshape(1)
    ys = _experts(eid, used, xs, w_gate, w_up, w_down, (), l, tm, 0)
    ys = _experts(eid, used, xs, w_gate, w_up, w_down, (ys,), l, tm, 1)
    return _combine(pos, x2, ys, mod_t, l, geo)


def _final_kernel(x_ref, g_ref, o_ref):
    o_ref[...] = _rms(x_ref[...]) * g_ref[...]


def _final_norm(x3, g, s_rows):
    b, lc, d = x3.shape
    tm = ROW_BLOCK
    return pl.pallas_call(
        _final_kernel,
        grid=(b, s_rows // tm),
        in_specs=[
            pl.BlockSpec((None, tm, d), lambda bi, i: (bi, i, 0)),
            pl.BlockSpec((1, d), lambda bi, i: (0, 0)),
        ],
        out_specs=pl.BlockSpec((None, tm, d), lambda bi, i: (bi, i, 0)),
        out_shape=jax.ShapeDtypeStruct((b, s_rows, d), F32),
        compiler_params=_params("arbitrary", "arbitrary"),
        name="final_norm",
    )(x3, g.reshape(1, d))


def _rope_tables(c_rows, s_rows):
    rows = s_rows // GRID_W
    row = jnp.broadcast_to(jnp.arange(rows)[:, None], (rows, GRID_W)).reshape(-1).astype(F32)
    col = jnp.broadcast_to(jnp.arange(GRID_W)[None, :], (rows, GRID_W)).reshape(-1).astype(F32)
    half = QK_ROPE // 2
    inv = ROPE_THETA ** (-jnp.arange(0, half, 2, dtype=F32) / half)
    ang_r = row[:, None] * inv
    ang_c = col[:, None] * inv
    ang = jnp.concatenate([ang_r, ang_r, ang_c, ang_c], axis=-1)
    ang = jnp.concatenate([ang, jnp.zeros((c_rows, QK_ROPE), F32)], axis=0)
    ang = jnp.concatenate([ang, ang], axis=-1)
    lane = jnp.arange(LANES)
    sign = jnp.where((lane % 32) < 16, -1.0, 1.0).astype(F32)
    return jnp.cos(ang), jnp.sin(ang) * sign


def kernel(x, c, ctx, c_ctx, w_mod, b_mod, norm_mix_g, norm_ffn_g, w_in, kv_norm_g, w_ukv, conv_w, conv_b,
           rg_wa, rg_ba, rg_wi, rg_bi, rg_lambda, w_o_mla, w_o_rnn, w_out, w_router, router_bias,
           w_gate, w_up, w_down, final_norm_g):
    b, s_rows, d = x.shape
    c_rows = ctx.shape[1]
    depth = w_mod.shape[0]
    kv_rank = kv_norm_g.shape[-1]
    lc = c_rows + s_rows
    m = b * lc
    assert c_rows % ROW_BLOCK == 0 and s_rows % c_rows == 0 and b < 16
    geo = {"nt": lc // ROW_BLOCK, "nctx": c_rows // ROW_BLOCK, "nb": b}

    nq = N_HEADS * QK_HEAD
    wq = w_in[:, :, :nq].reshape(depth, d, N_HEADS, QK_HEAD)
    off_qr = N_HEADS * QK_NOPE
    off_ckv = off_qr + N_HEADS * QK_ROPE
    off_kr = off_ckv + kv_rank
    off_rx = -(-(off_kr + LANES) // d) * d
    off_ry, off_gm, off_gr = off_rx + d, off_rx + 2 * d, off_rx + 3 * d
    tail = nq + kv_rank + QK_ROPE
    w_in_p = jnp.concatenate([
        wq[..., :QK_NOPE].reshape(depth, d, -1),
        wq[..., QK_NOPE:].reshape(depth, d, -1),
        w_in[:, :, nq:tail],
        jnp.zeros((depth, d, off_rx - off_kr - QK_ROPE), w_in.dtype),
        w_in[:, :, tail:],
    ], axis=-1).astype(BF16)
    wkv = w_ukv.reshape(depth, kv_rank, N_HEADS, QK_NOPE + V_HEAD)
    w_ukv_p = jnp.concatenate([wkv[..., :QK_NOPE].reshape(depth, kv_rank, -1),
                               wkv[..., QK_NOPE:].reshape(depth, kv_rank, -1)], axis=-1).astype(BF16)
    rg_w = (0.5 * jnp.concatenate([rg_wa, rg_wi], axis=-1)).astype(BF16)
    rg_p = jnp.stack([rg_ba, rg_bi, rg_lambda], axis=2)
    w_o_mla_b, w_o_rnn_b, w_out_b = w_o_mla.astype(BF16), w_o_rnn.astype(BF16), w_out.astype(BF16)
    w_gate_b, w_up_b, w_down_b = w_gate.astype(BF16), w_up.astype(BF16), w_down.astype(BF16)
    wr_t = w_router.T
    wr_hi = wr_t.astype(BF16)
    wr_lo = (wr_t - wr_hi.astype(F32)).astype(BF16)
    rbias = router_bias.reshape(-1, 1).astype(F32)
    norm_mix = norm_mix_g.reshape(depth, 1, d)
    norm_ffn = norm_ffn_g.reshape(depth, 1, d)
    kv_g = kv_norm_g.reshape(depth, 1, kv_rank)
    conv_b3 = conv_b.reshape(depth, 1, d)
    cos_t, sin_t = _rope_tables(c_rows, s_rows)

    cc = jnp.concatenate([c, c_ctx[None, :], jnp.zeros((16 - b - 1, d), F32)], axis=0)
    mod_t = _mod_table(cc, w_mod, b_mod).reshape(depth, 16, N_MOD, d).transpose(0, 2, 1, 3)

    x2 = jnp.concatenate([x, ctx], axis=1).reshape(m, d)
    for l in range(depth):
        p = _inproj(x2, norm_mix, mod_t, w_in_p, l, geo)
        p3 = p.reshape(b, lc, -1)
        kv = _kvup(p, kv_g, w_ukv_p, l, off_ckv)
        o = _attention(p3, kv.reshape(b, lc, -1), cos_t, sin_t, s_rows, off_qr, off_kr)
        hd = _rglru(p3, conv_w, conv_b3, rg_w, rg_p, l, geo, off_rx)
        z = _merge(o.reshape(m, -1), hd.reshape(2, m, d), p, w_o_mla_b, w_o_rnn_b, l, off_ry, off_gm, off_gr)
        x2 = _outproj(z, w_out_b, x2, mod_t, l, geo)
        x2 = _moe_ffn(x2, norm_ffn, mod_t, wr_hi, wr_lo, rbias, w_gate_b, w_up_b, w_down_b, l, geo)
    return _final_norm(x2.reshape(b, lc, d), final_norm_g, s_rows)
```

```python
import functools

import jax
import jax.numpy as jnp
from jax import lax
from jax.experimental import pallas as pl
from jax.experimental.pallas import tpu as pltpu

N_HEADS = 16
QK_NOPE = 128
QK_ROPE = 64
QK_HEAD = QK_NOPE + QK_ROPE
V_HEAD = 128
GRID_W = 64
ROPE_THETA = 10000.0
RG_BLOCKS = 8
CONV_W = 4
CONV_PAD_L = 2
RG_C = 8.0
N_GROUPS = 4
EPS = 1e-6
TINY = 1e-30
N_MOD = 6

LANES = 128
SUBLANES = 8
ROW_BLOCK = 256
KEY_CHUNK = 256
DMA_UNROLL = 16
SCAN_PAD = 4
LOG2E = 1.4426950408889634
VMEM_LIMIT = 56 * 1024 * 1024

F32 = jnp.float32
BF16 = jnp.bfloat16


def _pick(n, candidates):
    for c in candidates:
        if n % c == 0:
            return c
    raise ValueError(f"no tile of {candidates} divides {n}")


def _params(*sem):
    return pltpu.CompilerParams(dimension_semantics=sem, vmem_limit_bytes=VMEM_LIMIT)


def _mod_row(blk, nt, nctx, nb):
    return jnp.where(blk % nt >= nt - nctx, nb, blk // nt)


def _rms(x):
    return x * lax.rsqrt(jnp.mean(x * x, axis=-1, keepdims=True) + EPS)


def _mod_kernel(c_ref, w_ref, b_ref, o_ref):
    c = c_ref[...]
    s = (c * jax.nn.sigmoid(c)).astype(BF16)
    o_ref[...] = jnp.dot(s, w_ref[...].astype(BF16), preferred_element_type=F32) + b_ref[...]


def _mod_table(cc, w_mod, b_mod):
    depth, d, n = w_mod.shape
    tn = _pick(n, (1024, 512, 256, 128))
    return pl.pallas_call(
        _mod_kernel,
        grid=(depth, n // tn),
        in_specs=[
            pl.BlockSpec((16, d), lambda l, j: (0, 0)),
            pl.BlockSpec((None, d, tn), lambda l, j: (l, 0, j)),
            pl.BlockSpec((None, 1, tn), lambda l, j: (l, 0, j)),
        ],
        out_specs=pl.BlockSpec((None, 16, tn), lambda l, j: (l, 0, j)),
        out_shape=jax.ShapeDtypeStruct((depth, 16, n), F32),
        compiler_params=_params("arbitrary", "arbitrary"),
        name="mod_table",
    )(cc, w_mod, b_mod.reshape(depth, 1, n))


def _inproj_kernel(x_ref, g_ref, sh_ref, sc_ref, w_ref, o_ref, h_ref, *, tm, nt, nctx, nb):
    i = pl.program_id(0)

    @pl.when(pl.program_id(1) == 0)
    def _():
        for s in range(tm // ROW_BLOCK):
            rows = slice(s * ROW_BLOCK, (s + 1) * ROW_BLOCK)
            row = _mod_row(i * (tm // ROW_BLOCK) + s, nt, nctx, nb)
            y = _rms(x_ref[rows, :]) * g_ref[...]
            y = y * (1.0 + sc_ref[pl.ds(row, 1), :]) + sh_ref[pl.ds(row, 1), :]
            h_ref[rows, :] = y.astype(BF16)

    o_ref[...] = jnp.dot(h_ref[...], w_ref[...], preferred_element_type=F32).astype(BF16)


def _inproj(x2, norm_g, mod_t, w_in_p, l, geo):
    m, d = x2.shape
    n = w_in_p.shape[-1]
    tm = _pick(m, (1024, 512, 256))
    tn = _pick(n, (1024, 512, 256, 128))
    kern = functools.partial(_inproj_kernel, tm=tm, nt=geo["nt"], nctx=geo["nctx"], nb=geo["nb"])
    return pl.pallas_call(
        kern,
        grid=(m // tm, n // tn),
        in_specs=[
            pl.BlockSpec((tm, d), lambda i, j: (i, 0)),
            pl.BlockSpec((None, 1, d), lambda i, j: (l, 0, 0)),
            pl.BlockSpec((None, None, 16, d), lambda i, j: (l, 0, 0, 0)),
            pl.BlockSpec((None, None, 16, d), lambda i, j: (l, 1, 0, 0)),
            pl.BlockSpec((None, d, tn), lambda i, j: (l, 0, j)),
        ],
        out_specs=pl.BlockSpec((tm, tn), lambda i, j: (i, j)),
        out_shape=jax.ShapeDtypeStruct((m, n), BF16),
        scratch_shapes=[pltpu.VMEM((tm, d), BF16)],
        compiler_params=_params("arbitrary", "arbitrary"),
        name="inproj",
    )(x2, norm_g, mod_t, mod_t, w_in_p)


def _kvup_kernel(c_ref, g_ref, w_ref, o_ref, h_ref):
    @pl.when(pl.program_id(1) == 0)
    def _():
        h_ref[...] = (_rms(c_ref[...].astype(F32)) * g_ref[...]).astype(BF16)

    o_ref[...] = jnp.dot(h_ref[...], w_ref[...], preferred_element_type=F32).astype(BF16)


def _kvup(p, kv_g, w_ukv_p, l, off_ckv):
    m = p.shape[0]
    r, n = w_ukv_p.shape[1:]
    tm = _pick(m, (1024, 512, 256))
    tn = _pick(n, (2048, 1024, 512, 256))
    cb = off_ckv // r
    return pl.pallas_call(
        _kvup_kernel,
        grid=(m // tm, n // tn),
        in_specs=[
            pl.BlockSpec((tm, r), lambda i, j: (i, cb)),
            pl.BlockSpec((None, 1, r), lambda i, j: (l, 0, 0)),
            pl.BlockSpec((None, r, tn), lambda i, j: (l, 0, j)),
        ],
        out_specs=pl.BlockSpec((tm, tn), lambda i, j: (i, j)),
        out_shape=jax.ShapeDtypeStruct((m, n), BF16),
        scratch_shapes=[pltpu.VMEM((tm, r), BF16)],
        compiler_params=_params("arbitrary", "arbitrary"),
        name="kvup",
    )(p, kv_g, w_ukv_p)


def _rope(x, cos, sin_signed):
    lane = lax.broadcasted_iota(jnp.int32, x.shape, 1)
    first = (lane % 32) < 16
    rot = jnp.where(first, pltpu.roll(x, LANES - 16, 1), pltpu.roll(x, 16, 1))
    return x * cos + rot * sin_signed


def _key_chunks(n):
    return [(k0, min(KEY_CHUNK, n - k0)) for k0 in range(0, n, KEY_CHUNK)]


def _attn_kernel(qn_ref, qr_ref, kn_ref, kr_ref, v_ref, cos_ref, sin_ref, o_in_ref, o_ref, kcat_ref,
                 *, tq, lc):
    del o_in_ref
    qi = pl.program_id(2)

    @pl.when(qi == 0)
    def _():
        kr = _rope(kr_ref[...].astype(F32), cos_ref[...], sin_ref[...])
        for j in range(2):
            kcat_ref[j, :, 0:QK_NOPE] = kn_ref[:, j * QK_NOPE:(j + 1) * QK_NOPE]
            krj = kr if j == 0 else pltpu.roll(kr, QK_ROPE, 1)
            kcat_ref[j, :, QK_NOPE:QK_NOPE + LANES] = krj.astype(BF16)

    scale = QK_HEAD ** -0.5 * LOG2E
    r0 = pl.multiple_of(qi * tq, tq)
    qr = _rope(qr_ref[...].astype(F32), cos_ref[pl.ds(r0, tq), :], sin_ref[pl.ds(r0, tq), :])
    qr = (qr * scale).astype(BF16)
    qn = (qn_ref[...].astype(F32) * scale).astype(BF16)
    nt_dims = (((1,), (1,)), ((), ()))

    for j in range(2):
        q = jnp.concatenate([qn[:, j * QK_NOPE:(j + 1) * QK_NOPE], qr], axis=1)
        m = part = acc = None
        for k0, kn in _key_chunks(lc):
            s = lax.dot_general(q, kcat_ref[j, k0:k0 + kn, :], nt_dims, preferred_element_type=F32)
            smax = jnp.max(s, axis=-1, keepdims=True)
            m_new = smax if m is None else jnp.maximum(m, smax)
            p = jnp.exp2(s - m_new)
            psum = p[:, 0:LANES]
            for c in range(1, kn // LANES):
                psum = psum + p[:, c * LANES:(c + 1) * LANES]
            pv = jnp.dot(p.astype(BF16), v_ref[k0:k0 + kn, j * V_HEAD:(j + 1) * V_HEAD],
                         preferred_element_type=F32)
            if m is None:
                part, acc = psum, pv
            else:
                alpha = jnp.exp2(m - m_new)
                part = alpha * part + psum
                acc = alpha * acc + pv
            m = m_new
        den = jnp.sum(part, axis=-1, keepdims=True)
        o_ref[:, j * V_HEAD:(j + 1) * V_HEAD] = (acc / den).astype(BF16)


def _attn_ctx_kernel(qn_ref, qr_ref, kn_ref, kr_ref, v_ref, o_in_ref, o_ref):
    del o_in_ref
    scale = QK_HEAD ** -0.5
    qr = (qr_ref[...].astype(F32) * scale).astype(BF16)
    qn = (qn_ref[...].astype(F32) * scale).astype(BF16)
    kr = kr_ref[...].astype(F32)
    for j in range(2):
        krj = kr if j == 0 else pltpu.roll(kr, QK_ROPE, 1)
        k = jnp.concatenate([kn_ref[:, j * QK_NOPE:(j + 1) * QK_NOPE], krj.astype(BF16)], axis=1)
        q = jnp.concatenate([qn[:, j * QK_NOPE:(j + 1) * QK_NOPE], qr], axis=1)
        s = lax.dot_general(q, k, (((1,), (1,)), ((), ())), preferred_element_type=F32)
        p = jnp.exp(s - jnp.max(s, axis=-1, keepdims=True))
        den = jnp.sum(p, axis=-1, keepdims=True)
        o = jnp.dot(p.astype(BF16), v_ref[:, j * V_HEAD:(j + 1) * V_HEAD], preferred_element_type=F32)
        o_ref[:, j * V_HEAD:(j + 1) * V_HEAD] = (o / den).astype(BF16)


def _attention(p3, kv3, cos_t, sin_t, s_rows, off_qr, off_kr):
    b, lc, _ = p3.shape
    c_rows = lc - s_rows
    hp = N_HEADS // 2
    tq = _pick(s_rows, (512, 256))
    qrb = off_qr // LANES
    krb = off_kr // LANES
    o = pl.pallas_call(
        functools.partial(_attn_kernel, tq=tq, lc=lc),
        grid=(b, hp, s_rows // tq),
        in_specs=[
            pl.BlockSpec((None, tq, 2 * QK_NOPE), lambda bi, h, q: (bi, q, h)),
            pl.BlockSpec((None, tq, LANES), lambda bi, h, q: (bi, q, qrb + h)),
            pl.BlockSpec((None, lc, 2 * QK_NOPE), lambda bi, h, q: (bi, 0, h)),
            pl.BlockSpec((None, lc, LANES), lambda bi, h, q: (bi, 0, krb)),
            pl.BlockSpec((None, lc, 2 * V_HEAD), lambda bi, h, q: (bi, 0, hp + h)),
            pl.BlockSpec((lc, LANES), lambda bi, h, q: (0, 0)),
            pl.BlockSpec((lc, LANES), lambda bi, h, q: (0, 0)),
            pl.BlockSpec(memory_space=pl.ANY),
        ],
        out_specs=pl.BlockSpec((None, tq, 2 * V_HEAD), lambda bi, h, q: (bi, q, h)),
        out_shape=jax.ShapeDtypeStruct((b, lc, N_HEADS * V_HEAD), BF16),
        scratch_shapes=[pltpu.VMEM((2, lc, QK_NOPE + LANES), BF16)],
        input_output_aliases={7: 0},
        compiler_params=_params("arbitrary", "arbitrary", "arbitrary"),
        name="attention",
    )(p3, p3, kv3, p3, kv3, cos_t, sin_t, jnp.zeros((b, lc, N_HEADS * V_HEAD), BF16))
    cb = s_rows // c_rows
    return pl.pallas_call(
        _attn_ctx_kernel,
        grid=(b, hp),
        in_specs=[
            pl.BlockSpec((None, c_rows, 2 * QK_NOPE), lambda bi, h: (bi, cb, h)),
            pl.BlockSpec((None, c_rows, LANES), lambda bi, h: (bi, cb, qrb + h)),
            pl.BlockSpec((None, c_rows, 2 * QK_NOPE), lambda bi, h: (bi, cb, h)),
            pl.BlockSpec((None, c_rows, LANES), lambda bi, h: (bi, cb, krb)),
            pl.BlockSpec((None, c_rows, 2 * V_HEAD), lambda bi, h: (bi, cb, hp + h)),
            pl.BlockSpec(memory_space=pl.ANY),
        ],
        out_specs=pl.BlockSpec((None, c_rows, 2 * V_HEAD), lambda bi, h: (bi, cb, h)),
        out_shape=jax.ShapeDtypeStruct(o.shape, BF16),
        input_output_aliases={5: 0},
        compiler_params=_params("arbitrary", "arbitrary"),
        name="attention_ctx",
    )(p3, p3, kv3, p3, kv3, o)


def _scan_tile(a_ref, b_ref, h_ref, o_ref, *, seg, pitch, reverse):
    nslab = a_ref.shape[0]

    def body(g, carry):
        row = (seg - 1 - g) if reverse else g
        idx = pl.ds(row, SUBLANES, stride=pitch)
        out = []
        for c in range(nslab):
            av = a_ref[c, idx, :]
            hc = av * carry[2 * c] + b_ref[c, idx, :]
            ac = av * carry[2 * c + 1]
            b_ref[c, idx, :] = hc
            a_ref[c, idx, :] = ac
            out += [hc, ac]
        return tuple(out)

    init = (jnp.zeros((SUBLANES, LANES), F32), jnp.ones((SUBLANES, LANES), F32)) * nslab
    fin = lax.fori_loop(0, seg, body, init)
    sub = lax.broadcasted_iota(jnp.int32, (SUBLANES, LANES), 0)
    for c in range(nslab):
        lanes = slice(c * LANES, (c + 1) * LANES)
        h_end, a_end = fin[2 * c], fin[2 * c + 1]
        cin = h_ref[:, lanes]
        for s in (range(SUBLANES - 1, 0, -1) if reverse else range(SUBLANES - 1)):
            nxt = a_end * cin + h_end
            if reverse:
                cin = jnp.where(sub == s - 1, pltpu.roll(nxt, SUBLANES - 1, 0), cin)
            else:
                cin = jnp.where(sub == s + 1, pltpu.roll(nxt, 1, 0), cin)
        out = a_end * cin + h_end
        edge = out[0:1, :] if reverse else out[SUBLANES - 1:SUBLANES, :]
        h_ref[:, lanes] = jnp.broadcast_to(edge, (SUBLANES, LANES))
        for s in range(SUBLANES):
            rows = slice(s * pitch, s * pitch + seg)
            h = b_ref[c, rows, :] + a_ref[c, rows, :] * cin[s:s + 1, :]
            o_ref[s * seg:(s + 1) * seg, lanes] = h.astype(BF16)


def _rglru_kernel(x_ref, xp_ref, xn_ref, cw_ref, cb_ref, w_ref, rp_ref, o_ref, a_ref, b_ref, h_ref,
                  *, tt, nt, nctx, d, bw):
    dr = pl.program_id(0)
    i = pl.program_id(2)
    t = _tile_of(dr, i, nt, nctx)
    nlat = nt - nctx
    seg = tt // SUBLANES
    pitch = a_ref.shape[1] // SUBLANES
    first = jnp.logical_or(t == 0, t == nlat)
    last = jnp.logical_or(t == nlat - 1, t == nt - 1)

    @pl.when(i == 0)
    def _():
        h_ref[...] = jnp.zeros_like(h_ref)

    x = x_ref[...]
    rr = lax.broadcasted_iota(jnp.int32, (tt, tt), 0)
    cc = lax.broadcasted_iota(jnp.int32, (tt, tt), 1)
    offsets = [k - CONV_PAD_L for k in range(CONV_W) if k != CONV_PAD_L]
    shift = jnp.concatenate([(cc == rr + o).astype(BF16) for o in offsets], axis=0)
    taps = jnp.dot(shift, x, preferred_element_type=F32)
    xr = cb_ref[...] + x.astype(F32) * cw_ref[CONV_PAD_L:CONV_PAD_L + 1, :]
    for j, o in enumerate(offsets):
        xr = xr + taps[j * tt:(j + 1) * tt, :] * cw_ref[o + CONV_PAD_L:o + CONV_PAD_L + 1, :]
    prev = jnp.where(first, 0.0, xp_ref[...].astype(F32)[SUBLANES:, :])
    nxt = jnp.where(last, 0.0, xn_ref[...].astype(F32)[:SUBLANES, :])
    sub = lax.broadcasted_iota(jnp.int32, prev.shape, 0)
    head = jnp.zeros_like(prev)
    tail = jnp.zeros_like(prev)
    for o in offsets:
        w = cw_ref[o + CONV_PAD_L:o + CONV_PAD_L + 1, :]
        if o < 0:
            head = head + jnp.where(sub < -o, pltpu.roll(prev, -o, 0), 0.0) * w
        else:
            tail = tail + jnp.where(sub >= SUBLANES - o, pltpu.roll(nxt, SUBLANES - o, 0), 0.0) * w
    xr = jnp.concatenate([xr[0:SUBLANES] + head, xr[SUBLANES:tt - SUBLANES], xr[tt - SUBLANES:] + tail], axis=0)
    xb = xr.astype(BF16)

    hba = 0.5 * rp_ref[0:1, :]
    hbi = 0.5 * rp_ref[1:2, :]
    ca = (-0.5 * RG_C * LOG2E) * jax.nn.softplus(-rp_ref[2:3, :])
    hx = 0.5 * xr
    for n in range(d // bw):
        cols = slice(n * bw, (n + 1) * bw)
        g = jnp.dot(xb[:, cols], w_ref[n], preferred_element_type=F32)
        ta = jnp.tanh(g[:, :bw] + hba[:, cols])
        ti = jnp.tanh(g[:, bw:] + hbi[:, cols])
        a = jnp.exp2(ca[:, cols] * ta + ca[:, cols])
        y = 1.0 - a * a
        root = y * lax.rsqrt(jnp.maximum(y, TINY))
        bx = root * ((ti + 1.0) * hx[:, cols])
        for c in range(bw // LANES):
            slab = n * (bw // LANES) + c
            lanes = slice(c * LANES, (c + 1) * LANES)
            for s in range(SUBLANES):
                a_ref[slab, s * pitch:s * pitch + seg, :] = a[s * seg:(s + 1) * seg, lanes]
                b_ref[slab, s * pitch:s * pitch + seg, :] = bx[s * seg:(s + 1) * seg, lanes]

    @pl.when(dr == 0)
    def _():
        _scan_tile(a_ref, b_ref, h_ref, o_ref, seg=seg, pitch=pitch, reverse=False)

    @pl.when(dr == 1)
    def _():
        _scan_tile(a_ref, b_ref, h_ref, o_ref, seg=seg, pitch=pitch, reverse=True)


def _tile_of(dr, i, nt, nctx):
    nlat = nt - nctx
    fwd = jnp.where(i < nctx, nlat + i, i - nctx)
    back = jnp.where(i < nctx, nt - 1 - i, nlat - 1 - (i - nctx))
    return jnp.where(dr == 0, fwd, back)


def _rglru(p3, conv_w, conv_b, rg_w, rg_p, l, geo, off_rx):
    b, lc, _ = p3.shape
    d = conv_w.shape[-1]
    bw = d // RG_BLOCKS
    tt = ROW_BLOCK
    nt = lc // tt
    nctx = geo["nctx"]
    xb = off_rx // d
    hb = tt // 16
    nh = lc // 16

    def tile(dr, i):
        return _tile_of(dr, i, nt, nctx)

    kern = functools.partial(_rglru_kernel, tt=tt, nt=nt, nctx=nctx, d=d, bw=bw)
    return pl.pallas_call(
        kern,
        grid=(2, b, nt),
        in_specs=[
            pl.BlockSpec((None, tt, d), lambda dr, bi, i: (bi, tile(dr, i), xb)),
            pl.BlockSpec((None, 16, d), lambda dr, bi, i: (bi, jnp.maximum(tile(dr, i) * hb - 1, 0), xb)),
            pl.BlockSpec((None, 16, d), lambda dr, bi, i: (bi, jnp.minimum((tile(dr, i) + 1) * hb, nh - 1), xb)),
            pl.BlockSpec((None, CONV_W, d), lambda dr, bi, i: (l, 0, 0)),
            pl.BlockSpec((None, 1, d), lambda dr, bi, i: (l, 0, 0)),
            pl.BlockSpec((None, None, RG_BLOCKS, bw, 2 * bw), lambda dr, bi, i: (l, dr, 0, 0, 0)),
            pl.BlockSpec((None, None, 3, d), lambda dr, bi, i: (l, dr, 0, 0)),
        ],
        out_specs=pl.BlockSpec((None, None, tt, d), lambda dr, bi, i: (dr, bi, tile(dr, i), 0)),
        out_shape=jax.ShapeDtypeStruct((2, b, lc, d), BF16),
        scratch_shapes=[
            pltpu.VMEM((d // LANES, tt + SUBLANES * SCAN_PAD, LANES), F32),
            pltpu.VMEM((d // LANES, tt + SUBLANES * SCAN_PAD, LANES), F32),
            pltpu.VMEM((SUBLANES, d), F32),
        ],
        compiler_params=_params("arbitrary", "arbitrary", "arbitrary"),
        name="rglru",
    )(p3, p3, p3, conv_w, conv_b, rg_w, rg_p)


def _merge_kernel(o_ref, hf_ref, hb_ref, ry_ref, gm_ref, gr_ref, wm_ref, wr_ref, z_ref, r_ref):
    @pl.when(pl.program_id(1) == 0)
    def _():
        h = hf_ref[...].astype(F32) + hb_ref[...].astype(F32)
        r_ref[...] = (h * jax.nn.gelu(ry_ref[...].astype(F32))).astype(BF16)

    ym = jnp.dot(o_ref[...], wm_ref[...], preferred_element_type=F32)
    yr = jnp.dot(r_ref[...], wr_ref[...], preferred_element_type=F32)
    z = jax.nn.sigmoid(gm_ref[...].astype(F32)) * ym + jax.nn.sigmoid(gr_ref[...].astype(F32)) * yr
    z_ref[...] = z.astype(BF16)


def _merge(o2, h2d, p, w_o_mla, w_o_rnn, l, off_ry, off_gm, off_gr):
    m, d = o2.shape
    tm = ROW_BLOCK
    tn = d
    ryb = off_ry // d
    gmb = off_gm // tn
    grb = off_gr // tn
    return pl.pallas_call(
        _merge_kernel,
        grid=(m // tm, d // tn),
        in_specs=[
            pl.BlockSpec((tm, d), lambda i, j: (i, 0)),
            pl.BlockSpec((None, tm, d), lambda i, j: (0, i, 0)),
            pl.BlockSpec((None, tm, d), lambda i, j: (1, i, 0)),
            pl.BlockSpec((tm, d), lambda i, j: (i, ryb)),
            pl.BlockSpec((tm, tn), lambda i, j: (i, gmb + j)),
            pl.BlockSpec((tm, tn), lambda i, j: (i, grb + j)),
            pl.BlockSpec((None, d, tn), lambda i, j: (l, 0, j)),
            pl.BlockSpec((None, d, tn), lambda i, j: (l, 0, j)),
        ],
        out_specs=pl.BlockSpec((tm, tn), lambda i, j: (i, j)),
        out_shape=jax.ShapeDtypeStruct((m, d), BF16),
        scratch_shapes=[pltpu.VMEM((tm, d), BF16)],
        compiler_params=_params("arbitrary", "arbitrary"),
        name="merge",
    )(o2, h2d, h2d, p, p, p, w_o_mla, w_o_rnn)


def _outproj_kernel(z_ref, w_ref, x_ref, g_ref, o_ref, *, tm, nt, nctx, nb):
    i = pl.program_id(0)
    y = jnp.dot(z_ref[...], w_ref[...], preferred_element_type=F32)
    for s in range(tm // ROW_BLOCK):
        rows = slice(s * ROW_BLOCK, (s + 1) * ROW_BLOCK)
        row = _mod_row(i * (tm // ROW_BLOCK) + s, nt, nctx, nb)
        o_ref[rows, :] = x_ref[rows, :] + g_ref[pl.ds(row, 1), :] * y[rows, :]


def _outproj(z, w_out, x2, mod_t, l, geo):
    m, d = x2.shape
    tm = _pick(m, (1024, 512, 256))
    tn = _pick(d, (1024, 512, 256))
    kern = functools.partial(_outproj_kernel, tm=tm, nt=geo["nt"], nctx=geo["nctx"], nb=geo["nb"])
    return pl.pallas_call(
        kern,
        grid=(m // tm, d // tn),
        in_specs=[
            pl.BlockSpec((tm, d), lambda i, j: (i, 0)),
            pl.BlockSpec((None, d, tn), lambda i, j: (l, 0, j)),
            pl.BlockSpec((tm, tn), lambda i, j: (i, j)),
            pl.BlockSpec((None, None, 16, tn), lambda i, j: (l, 2, 0, j)),
        ],
        out_specs=pl.BlockSpec((tm, tn), lambda i, j: (i, j)),
        out_shape=jax.ShapeDtypeStruct((m, d), F32),
        input_output_aliases={2: 0},
        compiler_params=_params("arbitrary", "arbitrary"),
        name="outproj",
    )(z, w_out, x2, mod_t)


def _pairs(per):
    return [(a, c) for a in range(per) for c in range(a + 1, per)]


def _router(h, wr_hi, wr_lo, rb):
    hi = h.astype(BF16)
    lo = (h - hi.astype(F32)).astype(BF16)
    nt_dims = (((1,), (1,)), ((), ()))
    logits = (lax.dot_general(wr_hi, hi, nt_dims, preferred_element_type=F32)
              + lax.dot_general(wr_hi, lo, nt_dims, preferred_element_type=F32)
              + lax.dot_general(wr_lo, hi, nt_dims, preferred_element_type=F32))
    scores = jax.nn.sigmoid(logits)
    biased = scores + rb
    e = scores.shape[0]
    per = e // N_GROUPS
    rows_b = [biased[j:j + 1, :] for j in range(e)]
    rows_s = [scores[j:j + 1, :] for j in range(e)]
    gscore = []
    for g in range(N_GROUPS):
        r = rows_b[g * per:(g + 1) * per]
        best = None
        for a in range(per):
            for c in range(a + 1, per):
                pair = r[a] + r[c]
                best = pair if best is None else jnp.maximum(best, pair)
        gscore.append(best)
    gbest = gscore[0]
    gidx = jnp.zeros_like(gbest, dtype=jnp.int32)
    for g in range(1, N_GROUPS):
        better = gscore[g] > gbest
        gbest = jnp.where(better, gscore[g], gbest)
        gidx = jnp.where(better, g, gidx)
    sel = []
    for g in range(N_GROUPS):
        r = rows_b[g * per:(g + 1) * per]
        for a in range(per):
            rank = jnp.zeros_like(gidx)
            for c in range(per):
                if c == a:
                    continue
                ahead = (r[c] > r[a]) if c > a else (r[c] >= r[a])
                rank = rank + ahead.astype(jnp.int32)
            sel.append(jnp.logical_and(gidx == g, rank < 2))
    den = None
    for j in range(e):
        term = jnp.where(sel[j], rows_s[j], 0.0)
        den = term if den is None else den + term
    cls = jnp.zeros_like(gidx)
    w_lo = jnp.zeros_like(den)
    w_hi = jnp.zeros_like(den)
    pairs = _pairs(per)
    for g in range(N_GROUPS):
        for pi, (a, c) in enumerate(pairs):
            both = jnp.logical_and(sel[g * per + a], sel[g * per + c])
            cls = jnp.where(both, g * len(pairs) + pi, cls)
            w_lo = jnp.where(both, rows_s[g * per + a] / den, w_lo)
            w_hi = jnp.where(both, rows_s[g * per + c] / den, w_hi)
    return cls, w_lo, w_hi


def _ffnprep_kernel(x_ref, g_ref, sh_ref, sc_ref, wh_ref, wl_ref, rb_ref, hs_ref, meta_ref, cnt_ref, run_ref,
                    *, tm, nt, nctx, nb, d):
    i = pl.program_id(0)

    @pl.when(i == 0)
    def _():
        run_ref[...] = jnp.zeros_like(run_ref)

    hs = []
    for s in range(tm // ROW_BLOCK):
        rows = slice(s * ROW_BLOCK, (s + 1) * ROW_BLOCK)
        row = _mod_row(i * (tm // ROW_BLOCK) + s, nt, nctx, nb)
        y = _rms(x_ref[rows, :]) * g_ref[...]
        y = y * (1.0 + sc_ref[pl.ds(row, 1), :]) + sh_ref[pl.ds(row, 1), :]
        hs_ref[rows, 0:d] = y
        hs.append(y)
    h = jnp.concatenate(hs, axis=0) if len(hs) > 1 else hs[0]
    cls, w_lo, w_hi = _router(h, wh_ref[...], wl_ref[...], rb_ref[...])
    extra = jnp.concatenate([w_lo, w_hi, jnp.zeros((LANES - 2, tm), F32)], axis=0)
    hs_ref[:, d:d + LANES] = jnp.transpose(extra)

    nc = run_ref.shape[0]
    onehot = (lax.broadcasted_iota(jnp.int32, (nc, tm), 0) == cls).astype(F32)
    before = lax.broadcasted_iota(jnp.int32, (tm, tm), 0) < lax.broadcasted_iota(jnp.int32, (tm, tm), 1)
    prefix = jnp.dot(onehot.astype(BF16), before.astype(BF16), preferred_element_type=F32)
    run = run_ref[:, 0:1]
    rank = jnp.sum(onehot * (prefix + run), axis=0, keepdims=True).astype(jnp.int32)
    meta_ref[...] = jnp.concatenate([cls, rank, jnp.zeros((SUBLANES - 2, tm), jnp.int32)], axis=0)
    run_ref[...] = run_ref[...] + jnp.sum(onehot, axis=1, keepdims=True)
    cnt_ref[...] = run_ref[...]


def _ffnprep(x2, norm_g, mod_t, wr_hi, wr_lo, rbias, l, geo, ncls):
    m, d = x2.shape
    e = wr_hi.shape[0]
    tm = _pick(m, (512, 256))
    nc = -(-ncls // SUBLANES) * SUBLANES
    kern = functools.partial(_ffnprep_kernel, tm=tm, nt=geo["nt"], nctx=geo["nctx"], nb=geo["nb"], d=d)
    return pl.pallas_call(
        kern,
        grid=(m // tm,),
        in_specs=[
            pl.BlockSpec((tm, d), lambda i: (i, 0)),
            pl.BlockSpec((None, 1, d), lambda i: (l, 0, 0)),
            pl.BlockSpec((None, None, 16, d), lambda i: (l, 3, 0, 0)),
            pl.BlockSpec((None, None, 16, d), lambda i: (l, 4, 0, 0)),
            pl.BlockSpec((e, d), lambda i: (0, 0)),
            pl.BlockSpec((e, d), lambda i: (0, 0)),
            pl.BlockSpec((e, 1), lambda i: (0, 0)),
        ],
        out_specs=[
            pl.BlockSpec((tm, d + LANES), lambda i: (i, 0)),
            pl.BlockSpec((SUBLANES, tm), lambda i: (0, i)),
            pl.BlockSpec((nc, LANES), lambda i: (0, 0)),
        ],
        out_shape=[jax.ShapeDtypeStruct((m, d + LANES), F32), jax.ShapeDtypeStruct((SUBLANES, m), jnp.int32),
                   jax.ShapeDtypeStruct((nc, LANES), F32)],
        scratch_shapes=[pltpu.VMEM((nc, LANES), F32)],
        compiler_params=_params("arbitrary"),
        name="ffnprep",
    )(x2, norm_g, mod_t, mod_t, wr_hi, wr_lo, rbias)


def _row_copy(src_ref, src_row, dst_ref, dst_row, sem):
    return pltpu.make_async_copy(src_ref.at[pl.ds(src_row, 1)], dst_ref.at[pl.ds(dst_row, 1)], sem)


def _dispatch_kernel(pos_ref, hs_ref, xs_in_ref, xs_ref, sem, *, tm):
    del xs_in_ref
    base = pl.program_id(0) * tm

    def start(r, carry):
        _row_copy(hs_ref, r, xs_ref, pos_ref[base + r], sem).start()
        return carry

    def wait(r, carry):
        _row_copy(hs_ref, r, xs_ref, pos_ref[base + r], sem).wait()
        return carry

    lax.fori_loop(0, tm, start, 0, unroll=DMA_UNROLL)
    lax.fori_loop(0, tm, wait, 0, unroll=DMA_UNROLL)


def _dispatch(pos, hs, xs0):
    m, w = hs.shape
    tm = _pick(m, (512, 256))
    return pl.pallas_call(
        functools.partial(_dispatch_kernel, tm=tm),
        grid_spec=pltpu.PrefetchScalarGridSpec(
            num_scalar_prefetch=1,
            grid=(m // tm,),
            in_specs=[
                pl.BlockSpec((tm, w), lambda i, pos: (i, 0)),
                pl.BlockSpec(memory_space=pl.ANY),
            ],
            out_specs=pl.BlockSpec(memory_space=pl.ANY),
            scratch_shapes=[pltpu.SemaphoreType.DMA(())],
        ),
        out_shape=jax.ShapeDtypeStruct(xs0.shape, F32),
        input_output_aliases={2: 0},
        compiler_params=_params("arbitrary"),
        name="dispatch",
    )(pos, hs, xs0)


def _expert_kernel(eid_ref, used_ref, xs_ref, wg_ref, wu_ref, wd_ref, *rest, d, k):
    del eid_ref
    o_ref = rest[-1]
    t = pl.program_id(0)

    @pl.when(t < used_ref[0])
    def _():
        x = xs_ref[:, 0:d].astype(BF16)
        gate = jnp.dot(x, wg_ref[...], preferred_element_type=F32)
        up = jnp.dot(x, wu_ref[...], preferred_element_type=F32)
        act = (gate * jax.nn.sigmoid(gate) * up).astype(BF16)
        y = jnp.dot(act, wd_ref[...], preferred_element_type=F32)
        y = xs_ref[:, d + k:d + k + 1] * y
        o_ref[...] = y.astype(o_ref.dtype) if k == 0 else rest[0][...].astype(F32) + y

    @pl.when(t >= used_ref[0])
    def _():
        o_ref[...] = jnp.zeros_like(o_ref)


def _experts(eid, used, xs, w_gate, w_up, w_down, prev, l, tm, k):
    p, w = xs.shape
    d = w - LANES
    f = w_gate.shape[-1]
    nt = p // tm
    row_spec = pl.BlockSpec((tm, d), lambda t, eid, used: (t, 0))
    return pl.pallas_call(
        functools.partial(_expert_kernel, d=d, k=k),
        grid_spec=pltpu.PrefetchScalarGridSpec(
            num_scalar_prefetch=2,
            grid=(nt,),
            in_specs=[
                pl.BlockSpec((tm, w), lambda t, eid, used: (t, 0)),
                pl.BlockSpec((None, None, d, f), lambda t, eid, used: (l, eid[k * nt + t], 0, 0)),
                pl.BlockSpec((None, None, d, f), lambda t, eid, used: (l, eid[k * nt + t], 0, 0)),
                pl.BlockSpec((None, None, f, d), lambda t, eid, used: (l, eid[k * nt + t], 0, 0)),
            ] + [row_spec] * len(prev),
            out_specs=row_spec,
        ),
        out_shape=jax.ShapeDtypeStruct((p, d), BF16 if k == 0 else F32),
        compiler_params=_params("arbitrary"),
        name="experts",
    )(eid, used, xs, w_gate, w_up, w_down, *prev)


def _combine_kernel(pos_ref, x_ref, ys_ref, g_ref, o_ref, y_ref, sem, *, tm, nt, nctx, nb):
    i = pl.program_id(0)
    base = i * tm

    def start(r, carry):
        _row_copy(ys_ref, pos_ref[base + r], y_ref, r, sem).start()
        return carry

    def wait(r, carry):
        _row_copy(ys_ref, pos_ref[base + r], y_ref, r, sem).wait()
        return carry

    lax.fori_loop(0, tm, start, 0, unroll=DMA_UNROLL)
    lax.fori_loop(0, tm, wait, 0, unroll=DMA_UNROLL)
    for s in range(tm // ROW_BLOCK):
        rows = slice(s * ROW_BLOCK, (s + 1) * ROW_BLOCK)
        row = _mod_row(i * (tm // ROW_BLOCK) + s, nt, nctx, nb)
        o_ref[rows, :] = x_ref[rows, :] + g_ref[pl.ds(row, 1), :] * y_ref[rows, :]


def _combine(pos, x2, ys, mod_t, l, geo):
    m, d = x2.shape
    tm = _pick(m, (512, 256))
    kern = functools.partial(_combine_kernel, tm=tm, nt=geo["nt"], nctx=geo["nctx"], nb=geo["nb"])
    return pl.pallas_call(
        kern,
        grid_spec=pltpu.PrefetchScalarGridSpec(
            num_scalar_prefetch=1,
            grid=(m // tm,),
            in_specs=[
                pl.BlockSpec((tm, d), lambda i, pos: (i, 0)),
                pl.BlockSpec(memory_space=pl.ANY),
                pl.BlockSpec((None, None, 16, d), lambda i, pos: (l, 5, 0, 0)),
            ],
            out_specs=pl.BlockSpec((tm, d), lambda i, pos: (i, 0)),
            scratch_shapes=[pltpu.VMEM((tm, d), F32), pltpu.SemaphoreType.DMA(())],
        ),
        out_shape=jax.ShapeDtypeStruct((m, d), F32),
        input_output_aliases={1: 0},
        compiler_params=_params("arbitrary"),
        name="combine",
    )(pos, x2, ys, mod_t)


def _moe_ffn(x2, norm_g, mod_t, wr_hi, wr_lo, rbias, w_gate, w_up, w_down, l, geo):
    m, d = x2.shape
    ne = w_gate.shape[1]
    per = ne // N_GROUPS
    pairs = _pairs(per)
    ncls = N_GROUPS * len(pairs)
    tm = ROW_BLOCK
    nt = m // tm + ncls
    hs, meta, cnt = _ffnprep(x2, norm_g, mod_t, wr_hi, wr_lo, rbias, l, geo, ncls)

    count = cnt[:ncls, 0].astype(jnp.int32)
    tiles = (count + tm - 1) // tm
    tile_end = jnp.cumsum(tiles)
    row0 = (tile_end - tiles) * tm
    pos = row0[meta[0]] + meta[1]
    used = tile_end[-1]
    tidx = jnp.minimum(jnp.arange(nt, dtype=jnp.int32), used - 1)
    tile_cls = jnp.sum((tile_end[None, :] <= tidx[:, None]).astype(jnp.int32), axis=1)
    e_lo = jnp.array([g * per + a for g in range(N_GROUPS) for a, _ in pairs], jnp.int32)
    e_hi = jnp.array([g * per + c for g in range(N_GROUPS) for _, c in pairs], jnp.int32)
    eid = jnp.concatenate([e_lo[tile_cls], e_hi[tile_cls]])

    xs = _dispatch(pos, hs, jnp.zeros((nt * tm, d + LANES), F32))
    used = used.reshape(1)
    ys = _experts(eid, used, xs, w_gate, w_up, w_down, (), l, tm, 0)
    ys = _experts(eid, used, xs, w_gate, w_up, w_down, (ys,), l, tm, 1)
    return _combine(pos, x2, ys, mod_t, l, geo)


def _final_kernel(x_ref, g_ref, o_ref):
    o_ref[...] = _rms(x_ref[...]) * g_ref[...]


def _final_norm(x3, g, s_rows):
    b, lc, d = x3.shape
    tm = ROW_BLOCK
    return pl.pallas_call(
        _final_kernel,
        grid=(b, s_rows // tm),
        in_specs=[
            pl.BlockSpec((None, tm, d), lambda bi, i: (bi, i, 0)),
            pl.BlockSpec((1, d), lambda bi, i: (0, 0)),
        ],
        out_specs=pl.BlockSpec((None, tm, d), lambda bi, i: (bi, i, 0)),
        out_shape=jax.ShapeDtypeStruct((b, s_rows, d), F32),
        compiler_params=_params("arbitrary", "arbitrary"),
        name="final_norm",
    )(x3, g.reshape(1, d))


def _rope_tables(c_rows, s_rows):
    rows = s_rows // GRID_W
    row = jnp.broadcast_to(jnp.arange(rows)[:, None], (rows, GRID_W)).reshape(-1).astype(F32)
    col = jnp.broadcast_to(jnp.arange(GRID_W)[None, :], (rows, GRID_W)).reshape(-1).astype(F32)
    half = QK_ROPE // 2
    inv = ROPE_THETA ** (-jnp.arange(0, half, 2, dtype=F32) / half)
    ang_r = row[:, None] * inv
    ang_c = col[:, None] * inv
    ang = jnp.concatenate([ang_r, ang_r, ang_c, ang_c], axis=-1)
    ang = jnp.concatenate([ang, jnp.zeros((c_rows, QK_ROPE), F32)], axis=0)
    ang = jnp.concatenate([ang, ang], axis=-1)
    lane = jnp.arange(LANES)
    sign = jnp.where((lane % 32) < 16, -1.0, 1.0).astype(F32)
    return jnp.cos(ang), jnp.sin(ang) * sign


def kernel(x, c, ctx, c_ctx, w_mod, b_mod, norm_mix_g, norm_ffn_g, w_in, kv_norm_g, w_ukv, conv_w, conv_b,
           rg_wa, rg_ba, rg_wi, rg_bi, rg_lambda, w_o_mla, w_o_rnn, w_out, w_router, router_bias,
           w_gate, w_up, w_down, final_norm_g):
    b, s_rows, d = x.shape
    c_rows = ctx.shape[1]
    depth = w_mod.shape[0]
    kv_rank = kv_norm_g.shape[-1]
    lc = c_rows + s_rows
    m = b * lc
    assert c_rows % ROW_BLOCK == 0 and s_rows % c_rows == 0 and b < 16
    geo = {"nt": lc // ROW_BLOCK, "nctx": c_rows // ROW_BLOCK, "nb": b}

    nq = N_HEADS * QK_HEAD
    w_in_b = w_in.astype(BF16)
    wq = w_in_b[:, :, :nq].reshape(depth, d, N_HEADS, QK_HEAD)
    off_qr = N_HEADS * QK_NOPE
    off_ckv = off_qr + N_HEADS * QK_ROPE
    off_kr = off_ckv + kv_rank
    off_rx = -(-(off_kr + LANES) // d) * d
    off_ry, off_gm, off_gr = off_rx + d, off_rx + 2 * d, off_rx + 3 * d
    tail = nq + kv_rank + QK_ROPE
    w_in_p = jnp.concatenate([
        wq[..., :QK_NOPE].reshape(depth, d, -1),
        wq[..., QK_NOPE:].reshape(depth, d, -1),
        w_in_b[:, :, nq:tail],
        jnp.zeros((depth, d, off_rx - off_kr - QK_ROPE), BF16),
        w_in_b[:, :, tail:],
    ], axis=-1)
    wkv = w_ukv.reshape(depth, kv_rank, N_HEADS, QK_NOPE + V_HEAD)
    w_ukv_p = jnp.concatenate([wkv[..., :QK_NOPE].reshape(depth, kv_rank, -1),
                               wkv[..., QK_NOPE:].reshape(depth, kv_rank, -1)], axis=-1).astype(BF16)
    rg_w = (0.5 * jnp.concatenate([rg_wa, rg_wi], axis=-1)).astype(BF16)
    rg_p = jnp.stack([rg_ba, rg_bi, rg_lambda], axis=2)
    w_o_mla_b, w_o_rnn_b, w_out_b = w_o_mla.astype(BF16), w_o_rnn.astype(BF16), w_out.astype(BF16)
    w_gate_b, w_up_b, w_down_b = w_gate.astype(BF16), w_up.astype(BF16), w_down.astype(BF16)
    wr_t = w_router.T
    wr_hi = wr_t.astype(BF16)
    wr_lo = (wr_t - wr_hi.astype(F32)).astype(BF16)
    rbias = router_bias.reshape(-1, 1).astype(F32)
    norm_mix = norm_mix_g.reshape(depth, 1, d)
    norm_ffn = norm_ffn_g.reshape(depth, 1, d)
    kv_g = kv_norm_g.reshape(depth, 1, kv_rank)
    conv_b3 = conv_b.reshape(depth, 1, d)
    cos_t, sin_t = _rope_tables(c_rows, s_rows)

    cc = jnp.concatenate([c, c_ctx[None, :], jnp.zeros((16 - b - 1, d), F32)], axis=0)
    mod_t = _mod_table(cc, w_mod, b_mod).reshape(depth, 16, N_MOD, d).transpose(0, 2, 1, 3)

    x2 = jnp.concatenate([x, ctx], axis=1).reshape(m, d)
    for l in range(depth):
        p = _inproj(x2, norm_mix, mod_t, w_in_p, l, geo)
        p3 = p.reshape(b, lc, -1)
        kv = _kvup(p, kv_g, w_ukv_p, l, off_ckv)
        o = _attention(p3, kv.reshape(b, lc, -1), cos_t, sin_t, s_rows, off_qr, off_kr)
        hd = _rglru(p3, conv_w, conv_b3, rg_w, rg_p, l, geo, off_rx)
        z = _merge(o.reshape(m, -1), hd.reshape(2, m, d), p, w_o_mla_b, w_o_rnn_b, l, off_ry, off_gm, off_gr)
        x2 = _outproj(z, w_out_b, x2, mod_t, l, geo)
        x2 = _moe_ffn(x2, norm_ffn, mod_t, wr_hi, wr_lo, rbias, w_gate_b, w_up_b, w_down_b, l, geo)
    return _final_norm(x2.reshape(b, lc, d), final_norm_g, s_rows)
```

```python
import functools

import jax
import jax.numpy as jnp
from jax import lax
from jax.experimental import pallas as pl
from jax.experimental.pallas import tpu as pltpu

N_HEADS = 16
QK_NOPE = 128
QK_ROPE = 64
QK_HEAD = QK_NOPE + QK_ROPE
V_HEAD = 128
GRID_W = 64
ROPE_THETA = 10000.0
RG_BLOCKS = 8
CONV_W = 4
CONV_PAD_L = 2
RG_C = 8.0
N_GROUPS = 4
EPS = 1e-6
TINY = 1e-30
N_MOD = 6

LANES = 128
SUBLANES = 8
ROW_BLOCK = 256
KEY_CHUNK = 256
DMA_UNROLL = 16
SCAN_PAD = 4
LOG2E = 1.4426950408889634
VMEM_LIMIT = 56 * 1024 * 1024

F32 = jnp.float32
BF16 = jnp.bfloat16


def _pick(n, candidates):
    for c in candidates:
        if n % c == 0:
            return c
    raise ValueError(f"no tile of {candidates} divides {n}")


def _params(*sem):
    return pltpu.CompilerParams(dimension_semantics=sem, vmem_limit_bytes=VMEM_LIMIT)


def _mod_row(blk, nt, nctx, nb):
    return jnp.where(blk % nt >= nt - nctx, nb, blk // nt)


def _rms(x):
    return x * lax.rsqrt(jnp.mean(x * x, axis=-1, keepdims=True) + EPS)


def _mod_kernel(c_ref, w_ref, b_ref, o_ref):
    c = c_ref[...]
    s = (c * jax.nn.sigmoid(c)).astype(BF16)
    o_ref[...] = jnp.dot(s, w_ref[...].astype(BF16), preferred_element_type=F32) + b_ref[...]


def _mod_table(cc, w_mod, b_mod):
    depth, d, n = w_mod.shape
    tn = _pick(n, (1024, 512, 256, 128))
    return pl.pallas_call(
        _mod_kernel,
        grid=(depth, n // tn),
        in_specs=[
            pl.BlockSpec((16, d), lambda l, j: (0, 0)),
            pl.BlockSpec((None, d, tn), lambda l, j: (l, 0, j)),
            pl.BlockSpec((None, 1, tn), lambda l, j: (l, 0, j)),
        ],
        out_specs=pl.BlockSpec((None, 16, tn), lambda l, j: (l, 0, j)),
        out_shape=jax.ShapeDtypeStruct((depth, 16, n), F32),
        compiler_params=_params("arbitrary", "arbitrary"),
        name="mod_table",
    )(cc, w_mod, b_mod.reshape(depth, 1, n))


def _inproj_kernel(x_ref, g_ref, sh_ref, sc_ref, w_ref, o_ref, h_ref, *, tm, nt, nctx, nb):
    i = pl.program_id(0)

    @pl.when(pl.program_id(1) == 0)
    def _():
        for s in range(tm // ROW_BLOCK):
            rows = slice(s * ROW_BLOCK, (s + 1) * ROW_BLOCK)
            row = _mod_row(i * (tm // ROW_BLOCK) + s, nt, nctx, nb)
            y = _rms(x_ref[rows, :]) * g_ref[...]
            y = y * (1.0 + sc_ref[pl.ds(row, 1), :]) + sh_ref[pl.ds(row, 1), :]
            h_ref[rows, :] = y.astype(BF16)

    o_ref[...] = jnp.dot(h_ref[...], w_ref[...], preferred_element_type=F32).astype(BF16)


def _inproj(x2, norm_g, mod_t, w_in_p, l, geo):
    m, d = x2.shape
    n = w_in_p.shape[-1]
    tm = _pick(m, (1024, 512, 256))
    tn = _pick(n, (2048, 1024, 512, 256, 128))
    kern = functools.partial(_inproj_kernel, tm=tm, nt=geo["nt"], nctx=geo["nctx"], nb=geo["nb"])
    return pl.pallas_call(
        kern,
        grid=(m // tm, n // tn),
        in_specs=[
            pl.BlockSpec((tm, d), lambda i, j: (i, 0)),
            pl.BlockSpec((None, 1, d), lambda i, j: (l, 0, 0)),
            pl.BlockSpec((None, None, 16, d), lambda i, j: (l, 0, 0, 0)),
            pl.BlockSpec((None, None, 16, d), lambda i, j: (l, 1, 0, 0)),
            pl.BlockSpec((None, d, tn), lambda i, j: (l, 0, j)),
        ],
        out_specs=pl.BlockSpec((tm, tn), lambda i, j: (i, j)),
        out_shape=jax.ShapeDtypeStruct((m, n), BF16),
        scratch_shapes=[pltpu.VMEM((tm, d), BF16)],
        compiler_params=_params("arbitrary", "arbitrary"),
        name="inproj",
    )(x2, norm_g, mod_t, mod_t, w_in_p)


def _kvup_kernel(c_ref, g_ref, w_ref, o_ref, h_ref):
    @pl.when(pl.program_id(1) == 0)
    def _():
        h_ref[...] = (_rms(c_ref[...].astype(F32)) * g_ref[...]).astype(BF16)

    o_ref[...] = jnp.dot(h_ref[...], w_ref[...], preferred_element_type=F32).astype(BF16)


def _kvup(p, kv_g, w_ukv_p, l, off_ckv):
    m = p.shape[0]
    r, n = w_ukv_p.shape[1:]
    tm = _pick(m, (1024, 512, 256))
    tn = _pick(n, (2048, 1024, 512, 256))
    cb = off_ckv // r
    return pl.pallas_call(
        _kvup_kernel,
        grid=(m // tm, n // tn),
        in_specs=[
            pl.BlockSpec((tm, r), lambda i, j: (i, cb)),
            pl.BlockSpec((None, 1, r), lambda i, j: (l, 0, 0)),
            pl.BlockSpec((None, r, tn), lambda i, j: (l, 0, j)),
        ],
        out_specs=pl.BlockSpec((tm, tn), lambda i, j: (i, j)),
        out_shape=jax.ShapeDtypeStruct((m, n), BF16),
        scratch_shapes=[pltpu.VMEM((tm, r), BF16)],
        compiler_params=_params("arbitrary", "arbitrary"),
        name="kvup",
    )(p, kv_g, w_ukv_p)


def _rope(x, cos, sin_signed):
    lane = lax.broadcasted_iota(jnp.int32, x.shape, 1)
    first = (lane % 32) < 16
    rot = jnp.where(first, pltpu.roll(x, LANES - 16, 1), pltpu.roll(x, 16, 1))
    return x * cos + rot * sin_signed


def _key_chunks(n):
    return [(k0, min(KEY_CHUNK, n - k0)) for k0 in range(0, n, KEY_CHUNK)]


def _attn_kernel(qn_ref, qr_ref, kn_ref, kr_ref, v_ref, cos_ref, sin_ref, o_in_ref, o_ref, kcat_ref,
                 *, tq, lc):
    del o_in_ref
    qi = pl.program_id(2)

    @pl.when(qi == 0)
    def _():
        kr = _rope(kr_ref[...].astype(F32), cos_ref[...], sin_ref[...])
        for j in range(2):
            kcat_ref[j, :, 0:QK_NOPE] = kn_ref[:, j * QK_NOPE:(j + 1) * QK_NOPE]
            krj = kr if j == 0 else pltpu.roll(kr, QK_ROPE, 1)
            kcat_ref[j, :, QK_NOPE:QK_NOPE + LANES] = krj.astype(BF16)

    scale = QK_HEAD ** -0.5 * LOG2E
    r0 = pl.multiple_of(qi * tq, tq)
    qr = _rope(qr_ref[...].astype(F32), cos_ref[pl.ds(r0, tq), :], sin_ref[pl.ds(r0, tq), :])
    qr = (qr * scale).astype(BF16)
    qn = (qn_ref[...].astype(F32) * scale).astype(BF16)
    nt_dims = (((1,), (1,)), ((), ()))

    for j in range(2):
        q = jnp.concatenate([qn[:, j * QK_NOPE:(j + 1) * QK_NOPE], qr], axis=1)
        m = part = acc = None
        for k0, kn in _key_chunks(lc):
            s = lax.dot_general(q, kcat_ref[j, k0:k0 + kn, :], nt_dims, preferred_element_type=F32)
            smax = jnp.max(s, axis=-1, keepdims=True)
            m_new = smax if m is None else jnp.maximum(m, smax)
            p = jnp.exp2(s - m_new)
            psum = p[:, 0:LANES]
            for c in range(1, kn // LANES):
                psum = psum + p[:, c * LANES:(c + 1) * LANES]
            pv = jnp.dot(p.astype(BF16), v_ref[k0:k0 + kn, j * V_HEAD:(j + 1) * V_HEAD],
                         preferred_element_type=F32)
            if m is None:
                part, acc = psum, pv
            else:
                alpha = jnp.exp2(m - m_new)
                part = alpha * part + psum
                acc = alpha * acc + pv
            m = m_new
        den = jnp.sum(part, axis=-1, keepdims=True)
        o_ref[:, j * V_HEAD:(j + 1) * V_HEAD] = (acc / den).astype(BF16)


def _attn_ctx_kernel(qn_ref, qr_ref, kn_ref, kr_ref, v_ref, o_in_ref, o_ref):
    del o_in_ref
    scale = QK_HEAD ** -0.5
    qr = (qr_ref[...].astype(F32) * scale).astype(BF16)
    qn = (qn_ref[...].astype(F32) * scale).astype(BF16)
    kr = kr_ref[...].astype(F32)
    for j in range(2):
        krj = kr if j == 0 else pltpu.roll(kr, QK_ROPE, 1)
        k = jnp.concatenate([kn_ref[:, j * QK_NOPE:(j + 1) * QK_NOPE], krj.astype(BF16)], axis=1)
        q = jnp.concatenate([qn[:, j * QK_NOPE:(j + 1) * QK_NOPE], qr], axis=1)
        s = lax.dot_general(q, k, (((1,), (1,)), ((), ())), preferred_element_type=F32)
        p = jnp.exp(s - jnp.max(s, axis=-1, keepdims=True))
        den = jnp.sum(p, axis=-1, keepdims=True)
        o = jnp.dot(p.astype(BF16), v_ref[:, j * V_HEAD:(j + 1) * V_HEAD], preferred_element_type=F32)
        o_ref[:, j * V_HEAD:(j + 1) * V_HEAD] = (o / den).astype(BF16)


def _attention(p3, kv3, cos_t, sin_t, s_rows, off_qr, off_kr):
    b, lc, _ = p3.shape
    c_rows = lc - s_rows
    hp = N_HEADS // 2
    tq = _pick(s_rows, (512, 256))
    qrb = off_qr // LANES
    krb = off_kr // LANES
    o = pl.pallas_call(
        functools.partial(_attn_kernel, tq=tq, lc=lc),
        grid=(b, hp, s_rows // tq),
        in_specs=[
            pl.BlockSpec((None, tq, 2 * QK_NOPE), lambda bi, h, q: (bi, q, h)),
            pl.BlockSpec((None, tq, LANES), lambda bi, h, q: (bi, q, qrb + h)),
            pl.BlockSpec((None, lc, 2 * QK_NOPE), lambda bi, h, q: (bi, 0, h)),
            pl.BlockSpec((None, lc, LANES), lambda bi, h, q: (bi, 0, krb)),
            pl.BlockSpec((None, lc, 2 * V_HEAD), lambda bi, h, q: (bi, 0, hp + h)),
            pl.BlockSpec((lc, LANES), lambda bi, h, q: (0, 0)),
            pl.BlockSpec((lc, LANES), lambda bi, h, q: (0, 0)),
            pl.BlockSpec(memory_space=pl.ANY),
        ],
        out_specs=pl.BlockSpec((None, tq, 2 * V_HEAD), lambda bi, h, q: (bi, q, h)),
        out_shape=jax.ShapeDtypeStruct((b, lc, N_HEADS * V_HEAD), BF16),
        scratch_shapes=[pltpu.VMEM((2, lc, QK_NOPE + LANES), BF16)],
        input_output_aliases={7: 0},
        compiler_params=_params("arbitrary", "arbitrary", "arbitrary"),
        name="attention",
    )(p3, p3, kv3, p3, kv3, cos_t, sin_t, jnp.zeros((b, lc, N_HEADS * V_HEAD), BF16))
    cb = s_rows // c_rows
    return pl.pallas_call(
        _attn_ctx_kernel,
        grid=(b, hp),
        in_specs=[
            pl.BlockSpec((None, c_rows, 2 * QK_NOPE), lambda bi, h: (bi, cb, h)),
            pl.BlockSpec((None, c_rows, LANES), lambda bi, h: (bi, cb, qrb + h)),
            pl.BlockSpec((None, c_rows, 2 * QK_NOPE), lambda bi, h: (bi, cb, h)),
            pl.BlockSpec((None, c_rows, LANES), lambda bi, h: (bi, cb, krb)),
            pl.BlockSpec((None, c_rows, 2 * V_HEAD), lambda bi, h: (bi, cb, hp + h)),
            pl.BlockSpec(memory_space=pl.ANY),
        ],
        out_specs=pl.BlockSpec((None, c_rows, 2 * V_HEAD), lambda bi, h: (bi, cb, h)),
        out_shape=jax.ShapeDtypeStruct(o.shape, BF16),
        input_output_aliases={5: 0},
        compiler_params=_params("arbitrary", "arbitrary"),
        name="attention_ctx",
    )(p3, p3, kv3, p3, kv3, o)


def _scan_tile(a_ref, b_ref, h_ref, o_ref, *, seg, pitch, reverse):
    nslab = a_ref.shape[0]

    def body(g, carry):
        row = (seg - 1 - g) if reverse else g
        idx = pl.ds(row, SUBLANES, stride=pitch)
        out = []
        for c in range(nslab):
            av = a_ref[c, idx, :]
            hc = av * carry[2 * c] + b_ref[c, idx, :]
            ac = av * carry[2 * c + 1]
            b_ref[c, idx, :] = hc
            a_ref[c, idx, :] = ac
            out += [hc, ac]
        return tuple(out)

    init = (jnp.zeros((SUBLANES, LANES), F32), jnp.ones((SUBLANES, LANES), F32)) * nslab
    fin = lax.fori_loop(0, seg, body, init)
    sub = lax.broadcasted_iota(jnp.int32, (SUBLANES, LANES), 0)
    for c in range(nslab):
        lanes = slice(c * LANES, (c + 1) * LANES)
        h_end, a_end = fin[2 * c], fin[2 * c + 1]
        cin = h_ref[:, lanes]
        for s in (range(SUBLANES - 1, 0, -1) if reverse else range(SUBLANES - 1)):
            nxt = a_end * cin + h_end
            if reverse:
                cin = jnp.where(sub == s - 1, pltpu.roll(nxt, SUBLANES - 1, 0), cin)
            else:
                cin = jnp.where(sub == s + 1, pltpu.roll(nxt, 1, 0), cin)
        out = a_end * cin + h_end
        edge = out[0:1, :] if reverse else out[SUBLANES - 1:SUBLANES, :]
        h_ref[:, lanes] = jnp.broadcast_to(edge, (SUBLANES, LANES))
        for s in range(SUBLANES):
            rows = slice(s * pitch, s * pitch + seg)
            h = b_ref[c, rows, :] + a_ref[c, rows, :] * cin[s:s + 1, :]
            o_ref[s * seg:(s + 1) * seg, lanes] = h.astype(BF16)


def _rglru_kernel(x_ref, xp_ref, xn_ref, cw_ref, cb_ref, w_ref, rp_ref, o_ref, a_ref, b_ref, h_ref,
                  *, tt, nt, nctx, d, bw):
    dr = pl.program_id(0)
    i = pl.program_id(2)
    t = _tile_of(dr, i, nt, nctx)
    nlat = nt - nctx
    seg = tt // SUBLANES
    pitch = a_ref.shape[1] // SUBLANES
    first = jnp.logical_or(t == 0, t == nlat)
    last = jnp.logical_or(t == nlat - 1, t == nt - 1)

    @pl.when(i == 0)
    def _():
        h_ref[...] = jnp.zeros_like(h_ref)

    x = x_ref[...]
    rr = lax.broadcasted_iota(jnp.int32, (tt, tt), 0)
    cc = lax.broadcasted_iota(jnp.int32, (tt, tt), 1)
    offsets = [k - CONV_PAD_L for k in range(CONV_W) if k != CONV_PAD_L]
    shift = jnp.concatenate([(cc == rr + o).astype(BF16) for o in offsets], axis=0)
    taps = jnp.dot(shift, x, preferred_element_type=F32)
    xr = cb_ref[...] + x.astype(F32) * cw_ref[CONV_PAD_L:CONV_PAD_L + 1, :]
    for j, o in enumerate(offsets):
        xr = xr + taps[j * tt:(j + 1) * tt, :] * cw_ref[o + CONV_PAD_L:o + CONV_PAD_L + 1, :]
    prev = jnp.where(first, 0.0, xp_ref[...].astype(F32)[SUBLANES:, :])
    nxt = jnp.where(last, 0.0, xn_ref[...].astype(F32)[:SUBLANES, :])
    sub = lax.broadcasted_iota(jnp.int32, prev.shape, 0)
    head = jnp.zeros_like(prev)
    tail = jnp.zeros_like(prev)
    for o in offsets:
        w = cw_ref[o + CONV_PAD_L:o + CONV_PAD_L + 1, :]
        if o < 0:
            head = head + jnp.where(sub < -o, pltpu.roll(prev, -o, 0), 0.0) * w
        else:
            tail = tail + jnp.where(sub >= SUBLANES - o, pltpu.roll(nxt, SUBLANES - o, 0), 0.0) * w
    xr = jnp.concatenate([xr[0:SUBLANES] + head, xr[SUBLANES:tt - SUBLANES], xr[tt - SUBLANES:] + tail], axis=0)
    xb = xr.astype(BF16)

    hba = 0.5 * rp_ref[0:1, :]
    hbi = 0.5 * rp_ref[1:2, :]
    ca = (-0.5 * RG_C * LOG2E) * jax.nn.softplus(-rp_ref[2:3, :])
    hx = 0.5 * xr
    for n in range(d // bw):
        cols = slice(n * bw, (n + 1) * bw)
        g = jnp.dot(xb[:, cols], w_ref[n], preferred_element_type=F32)
        ta = jnp.tanh(g[:, :bw] + hba[:, cols])
        ti = jnp.tanh(g[:, bw:] + hbi[:, cols])
        a = jnp.exp2(ca[:, cols] * ta + ca[:, cols])
        y = 1.0 - a * a
        root = y * lax.rsqrt(jnp.maximum(y, TINY))
        bx = root * ((ti + 1.0) * hx[:, cols])
        for c in range(bw // LANES):
            slab = n * (bw // LANES) + c
            lanes = slice(c * LANES, (c + 1) * LANES)
            for s in range(SUBLANES):
                a_ref[slab, s * pitch:s * pitch + seg, :] = a[s * seg:(s + 1) * seg, lanes]
                b_ref[slab, s * pitch:s * pitch + seg, :] = bx[s * seg:(s + 1) * seg, lanes]

    @pl.when(dr == 0)
    def _():
        _scan_tile(a_ref, b_ref, h_ref, o_ref, seg=seg, pitch=pitch, reverse=False)

    @pl.when(dr == 1)
    def _():
        _scan_tile(a_ref, b_ref, h_ref, o_ref, seg=seg, pitch=pitch, reverse=True)


def _tile_of(dr, i, nt, nctx):
    nlat = nt - nctx
    fwd = jnp.where(i < nctx, nlat + i, i - nctx)
    back = jnp.where(i < nctx, nt - 1 - i, nlat - 1 - (i - nctx))
    return jnp.where(dr == 0, fwd, back)


def _rglru(p3, conv_w, conv_b, rg_w, rg_p, l, geo, off_rx):
    b, lc, _ = p3.shape
    d = conv_w.shape[-1]
    bw = d // RG_BLOCKS
    tt = ROW_BLOCK
    nt = lc // tt
    nctx = geo["nctx"]
    xb = off_rx // d
    hb = tt // 16
    nh = lc // 16

    def tile(dr, i):
        return _tile_of(dr, i, nt, nctx)

    kern = functools.partial(_rglru_kernel, tt=tt, nt=nt, nctx=nctx, d=d, bw=bw)
    return pl.pallas_call(
        kern,
        grid=(2, b, nt),
        in_specs=[
            pl.BlockSpec((None, tt, d), lambda dr, bi, i: (bi, tile(dr, i), xb)),
            pl.BlockSpec((None, 16, d), lambda dr, bi, i: (bi, jnp.maximum(tile(dr, i) * hb - 1, 0), xb)),
            pl.BlockSpec((None, 16, d), lambda dr, bi, i: (bi, jnp.minimum((tile(dr, i) + 1) * hb, nh - 1), xb)),
            pl.BlockSpec((None, CONV_W, d), lambda dr, bi, i: (l, 0, 0)),
            pl.BlockSpec((None, 1, d), lambda dr, bi, i: (l, 0, 0)),
            pl.BlockSpec((None, None, RG_BLOCKS, bw, 2 * bw), lambda dr, bi, i: (l, dr, 0, 0, 0)),
            pl.BlockSpec((None, None, 3, d), lambda dr, bi, i: (l, dr, 0, 0)),
        ],
        out_specs=pl.BlockSpec((None, None, tt, d), lambda dr, bi, i: (dr, bi, tile(dr, i), 0)),
        out_shape=jax.ShapeDtypeStruct((2, b, lc, d), BF16),
        scratch_shapes=[
            pltpu.VMEM((d // LANES, tt + SUBLANES * SCAN_PAD, LANES), F32),
            pltpu.VMEM((d // LANES, tt + SUBLANES * SCAN_PAD, LANES), F32),
            pltpu.VMEM((SUBLANES, d), F32),
        ],
        compiler_params=_params("arbitrary", "arbitrary", "arbitrary"),
        name="rglru",
    )(p3, p3, p3, conv_w, conv_b, rg_w, rg_p)


def _merge_kernel(o_ref, hf_ref, hb_ref, ry_ref, gm_ref, gr_ref, wm_ref, wr_ref, z_ref, r_ref):
    @pl.when(pl.program_id(1) == 0)
    def _():
        h = hf_ref[...].astype(F32) + hb_ref[...].astype(F32)
        r_ref[...] = (h * jax.nn.gelu(ry_ref[...].astype(F32))).astype(BF16)

    ym = jnp.dot(o_ref[...], wm_ref[...], preferred_element_type=F32)
    yr = jnp.dot(r_ref[...], wr_ref[...], preferred_element_type=F32)
    z = ((jnp.tanh(0.5 * gm_ref[...].astype(F32)) + 1.0) * ym
         + (jnp.tanh(0.5 * gr_ref[...].astype(F32)) + 1.0) * yr) * 0.5
    z_ref[...] = z.astype(BF16)


def _merge(o2, h2d, p, w_o_mla, w_o_rnn, l, off_ry, off_gm, off_gr):
    m, d = o2.shape
    tm = ROW_BLOCK
    tn = d
    ryb = off_ry // d
    gmb = off_gm // tn
    grb = off_gr // tn
    return pl.pallas_call(
        _merge_kernel,
        grid=(m // tm, d // tn),
        in_specs=[
            pl.BlockSpec((tm, d), lambda i, j: (i, 0)),
            pl.BlockSpec((None, tm, d), lambda i, j: (0, i, 0)),
            pl.BlockSpec((None, tm, d), lambda i, j: (1, i, 0)),
            pl.BlockSpec((tm, d), lambda i, j: (i, ryb)),
            pl.BlockSpec((tm, tn), lambda i, j: (i, gmb + j)),
            pl.BlockSpec((tm, tn), lambda i, j: (i, grb + j)),
            pl.BlockSpec((None, d, tn), lambda i, j: (l, 0, j)),
            pl.BlockSpec((None, d, tn), lambda i, j: (l, 0, j)),
        ],
        out_specs=pl.BlockSpec((tm, tn), lambda i, j: (i, j)),
        out_shape=jax.ShapeDtypeStruct((m, d), BF16),
        scratch_shapes=[pltpu.VMEM((tm, d), BF16)],
        compiler_params=_params("arbitrary", "arbitrary"),
        name="merge",
    )(o2, h2d, h2d, p, p, p, w_o_mla, w_o_rnn)


def _outproj_kernel(z_ref, w_ref, x_ref, g_ref, o_ref, *, tm, nt, nctx, nb):
    i = pl.program_id(0)
    y = jnp.dot(z_ref[...], w_ref[...], preferred_element_type=F32)
    for s in range(tm // ROW_BLOCK):
        rows = slice(s * ROW_BLOCK, (s + 1) * ROW_BLOCK)
        row = _mod_row(i * (tm // ROW_BLOCK) + s, nt, nctx, nb)
        o_ref[rows, :] = x_ref[rows, :] + g_ref[pl.ds(row, 1), :] * y[rows, :]


def _outproj(z, w_out, x2, mod_t, l, geo):
    m, d = x2.shape
    tm = _pick(m, (1024, 512, 256))
    tn = _pick(d, (1024, 512, 256))
    kern = functools.partial(_outproj_kernel, tm=tm, nt=geo["nt"], nctx=geo["nctx"], nb=geo["nb"])
    return pl.pallas_call(
        kern,
        grid=(m // tm, d // tn),
        in_specs=[
            pl.BlockSpec((tm, d), lambda i, j: (i, 0)),
            pl.BlockSpec((None, d, tn), lambda i, j: (l, 0, j)),
            pl.BlockSpec((tm, tn), lambda i, j: (i, j)),
            pl.BlockSpec((None, None, 16, tn), lambda i, j: (l, 2, 0, j)),
        ],
        out_specs=pl.BlockSpec((tm, tn), lambda i, j: (i, j)),
        out_shape=jax.ShapeDtypeStruct((m, d), F32),
        input_output_aliases={2: 0},
        compiler_params=_params("arbitrary", "arbitrary"),
        name="outproj",
    )(z, w_out, x2, mod_t)


def _pairs(per):
    return [(a, c) for a in range(per) for c in range(a + 1, per)]


def _router(h, wr_hi, wr_lo, rb):
    hi = h.astype(BF16)
    lo = (h - hi.astype(F32)).astype(BF16)
    nt_dims = (((1,), (1,)), ((), ()))
    logits = (lax.dot_general(wr_hi, hi, nt_dims, preferred_element_type=F32)
              + lax.dot_general(wr_hi, lo, nt_dims, preferred_element_type=F32)
              + lax.dot_general(wr_lo, hi, nt_dims, preferred_element_type=F32))
    scores = jax.nn.sigmoid(logits)
    biased = scores + rb
    e = scores.shape[0]
    per = e // N_GROUPS
    rows_b = [biased[j:j + 1, :] for j in range(e)]
    rows_s = [scores[j:j + 1, :] for j in range(e)]
    gscore = []
    for g in range(N_GROUPS):
        r = rows_b[g * per:(g + 1) * per]
        best = None
        for a in range(per):
            for c in range(a + 1, per):
                pair = r[a] + r[c]
                best = pair if best is None else jnp.maximum(best, pair)
        gscore.append(best)
    gbest = gscore[0]
    gidx = jnp.zeros_like(gbest, dtype=jnp.int32)
    for g in range(1, N_GROUPS):
        better = gscore[g] > gbest
        gbest = jnp.where(better, gscore[g], gbest)
        gidx = jnp.where(better, g, gidx)
    sel = []
    for g in range(N_GROUPS):
        r = rows_b[g * per:(g + 1) * per]
        for a in range(per):
            rank = jnp.zeros_like(gidx)
            for c in range(per):
                if c == a:
                    continue
                ahead = (r[c] > r[a]) if c > a else (r[c] >= r[a])
                rank = rank + ahead.astype(jnp.int32)
            sel.append(jnp.logical_and(gidx == g, rank < 2))
    den = None
    for j in range(e):
        term = jnp.where(sel[j], rows_s[j], 0.0)
        den = term if den is None else den + term
    cls = jnp.zeros_like(gidx)
    w_lo = jnp.zeros_like(den)
    w_hi = jnp.zeros_like(den)
    pairs = _pairs(per)
    for g in range(N_GROUPS):
        for pi, (a, c) in enumerate(pairs):
            both = jnp.logical_and(sel[g * per + a], sel[g * per + c])
            cls = jnp.where(both, g * len(pairs) + pi, cls)
            w_lo = jnp.where(both, rows_s[g * per + a] / den, w_lo)
            w_hi = jnp.where(both, rows_s[g * per + c] / den, w_hi)
    return cls, w_lo, w_hi


def _ffnprep_kernel(x_ref, g_ref, sh_ref, sc_ref, wh_ref, wl_ref, rb_ref, hs_ref, meta_ref, cnt_ref, run_ref,
                    *, tm, nt, nctx, nb, d):
    i = pl.program_id(0)

    @pl.when(i == 0)
    def _():
        run_ref[...] = jnp.zeros_like(run_ref)

    hs = []
    for s in range(tm // ROW_BLOCK):
        rows = slice(s * ROW_BLOCK, (s + 1) * ROW_BLOCK)
        row = _mod_row(i * (tm // ROW_BLOCK) + s, nt, nctx, nb)
        y = _rms(x_ref[rows, :]) * g_ref[...]
        y = y * (1.0 + sc_ref[pl.ds(row, 1), :]) + sh_ref[pl.ds(row, 1), :]
        hs_ref[rows, 0:d] = y
        hs.append(y)
    h = jnp.concatenate(hs, axis=0) if len(hs) > 1 else hs[0]
    cls, w_lo, w_hi = _router(h, wh_ref[...], wl_ref[...], rb_ref[...])
    extra = jnp.concatenate([w_lo, w_hi, jnp.zeros((LANES - 2, tm), F32)], axis=0)
    hs_ref[:, d:d + LANES] = jnp.transpose(extra)

    nc = run_ref.shape[0]
    onehot = (lax.broadcasted_iota(jnp.int32, (nc, tm), 0) == cls).astype(F32)
    before = lax.broadcasted_iota(jnp.int32, (tm, tm), 0) < lax.broadcasted_iota(jnp.int32, (tm, tm), 1)
    prefix = jnp.dot(onehot.astype(BF16), before.astype(BF16), preferred_element_type=F32)
    run = run_ref[:, 0:1]
    rank = jnp.sum(onehot * (prefix + run), axis=0, keepdims=True).astype(jnp.int32)
    meta_ref[...] = jnp.concatenate([cls, rank, jnp.zeros((SUBLANES - 2, tm), jnp.int32)], axis=0)
    run_ref[...] = run_ref[...] + jnp.sum(onehot, axis=1, keepdims=True)
    cnt_ref[...] = run_ref[...]


def _ffnprep(x2, norm_g, mod_t, wr_hi, wr_lo, rbias, l, geo, ncls):
    m, d = x2.shape
    e = wr_hi.shape[0]
    tm = _pick(m, (512, 256))
    nc = -(-ncls // SUBLANES) * SUBLANES
    kern = functools.partial(_ffnprep_kernel, tm=tm, nt=geo["nt"], nctx=geo["nctx"], nb=geo["nb"], d=d)
    return pl.pallas_call(
        kern,
        grid=(m // tm,),
        in_specs=[
            pl.BlockSpec((tm, d), lambda i: (i, 0)),
            pl.BlockSpec((None, 1, d), lambda i: (l, 0, 0)),
            pl.BlockSpec((None, None, 16, d), lambda i: (l, 3, 0, 0)),
            pl.BlockSpec((None, None, 16, d), lambda i: (l, 4, 0, 0)),
            pl.BlockSpec((e, d), lambda i: (0, 0)),
            pl.BlockSpec((e, d), lambda i: (0, 0)),
            pl.BlockSpec((e, 1), lambda i: (0, 0)),
        ],
        out_specs=[
            pl.BlockSpec((tm, d + LANES), lambda i: (i, 0)),
            pl.BlockSpec((SUBLANES, tm), lambda i: (0, i)),
            pl.BlockSpec((nc, LANES), lambda i: (0, 0)),
        ],
        out_shape=[jax.ShapeDtypeStruct((m, d + LANES), F32), jax.ShapeDtypeStruct((SUBLANES, m), jnp.int32),
                   jax.ShapeDtypeStruct((nc, LANES), F32)],
        scratch_shapes=[pltpu.VMEM((nc, LANES), F32)],
        compiler_params=_params("arbitrary"),
        name="ffnprep",
    )(x2, norm_g, mod_t, mod_t, wr_hi, wr_lo, rbias)


def _row_copy(src_ref, src_row, dst_ref, dst_row, sem):
    return pltpu.make_async_copy(src_ref.at[pl.ds(src_row, 1)], dst_ref.at[pl.ds(dst_row, 1)], sem)


def _dispatch_kernel(pos_ref, hs_ref, xs_in_ref, xs_ref, sem, *, tm):
    del xs_in_ref
    base = pl.program_id(0) * tm

    def start(r, carry):
        _row_copy(hs_ref, r, xs_ref, pos_ref[base + r], sem).start()
        return carry

    def wait(r, carry):
        _row_copy(hs_ref, r, xs_ref, pos_ref[base + r], sem).wait()
        return carry

    lax.fori_loop(0, tm, start, 0, unroll=DMA_UNROLL)
    lax.fori_loop(0, tm, wait, 0, unroll=DMA_UNROLL)


def _dispatch(pos, hs, xs0):
    m, w = hs.shape
    tm = _pick(m, (512, 256))
    return pl.pallas_call(
        functools.partial(_dispatch_kernel, tm=tm),
        grid_spec=pltpu.PrefetchScalarGridSpec(
            num_scalar_prefetch=1,
            grid=(m // tm,),
            in_specs=[
                pl.BlockSpec((tm, w), lambda i, pos: (i, 0)),
                pl.BlockSpec(memory_space=pl.ANY),
            ],
            out_specs=pl.BlockSpec(memory_space=pl.ANY),
            scratch_shapes=[pltpu.SemaphoreType.DMA(())],
        ),
        out_shape=jax.ShapeDtypeStruct(xs0.shape, F32),
        input_output_aliases={2: 0},
        compiler_params=_params("arbitrary"),
        name="dispatch",
    )(pos, hs, xs0)


def _expert_kernel(eid_ref, used_ref, xs_ref, wg_ref, wu_ref, wd_ref, *rest, d, k):
    del eid_ref
    o_ref = rest[-1]
    t = pl.program_id(0)

    @pl.when(t < used_ref[0])
    def _():
        x = xs_ref[:, 0:d].astype(BF16)
        gate = jnp.dot(x, wg_ref[...], preferred_element_type=F32)
        up = jnp.dot(x, wu_ref[...], preferred_element_type=F32)
        act = (gate * jax.nn.sigmoid(gate) * up).astype(BF16)
        y = jnp.dot(act, wd_ref[...], preferred_element_type=F32)
        y = xs_ref[:, d + k:d + k + 1] * y
        o_ref[...] = y.astype(o_ref.dtype) if k == 0 else rest[0][...].astype(F32) + y

    @pl.when(t >= used_ref[0])
    def _():
        o_ref[...] = jnp.zeros_like(o_ref)


def _experts(eid, used, xs, w_gate, w_up, w_down, prev, l, tm, k):
    p, w = xs.shape
    d = w - LANES
    f = w_gate.shape[-1]
    nt = p // tm
    row_spec = pl.BlockSpec((tm, d), lambda t, eid, used: (t, 0))
    return pl.pallas_call(
        functools.partial(_expert_kernel, d=d, k=k),
        grid_spec=pltpu.PrefetchScalarGridSpec(
            num_scalar_prefetch=2,
            grid=(nt,),
            in_specs=[
                pl.BlockSpec((tm, w), lambda t, eid, used: (t, 0)),
                pl.BlockSpec((None, None, d, f), lambda t, eid, used: (l, eid[k * nt + t], 0, 0)),
                pl.BlockSpec((None, None, d, f), lambda t, eid, used: (l, eid[k * nt + t], 0, 0)),
                pl.BlockSpec((None, None, f, d), lambda t, eid, used: (l, eid[k * nt + t], 0, 0)),
            ] + [row_spec] * len(prev),
            out_specs=row_spec,
        ),
        out_shape=jax.ShapeDtypeStruct((p, d), BF16 if k == 0 else F32),
        compiler_params=_params("arbitrary"),
        name="experts",
    )(eid, used, xs, w_gate, w_up, w_down, *prev)


def _combine_kernel(pos_ref, x_ref, ys_ref, g_ref, o_ref, y_ref, sem, *, tm, nt, nctx, nb):
    i = pl.program_id(0)
    base = i * tm

    def start(r, carry):
        _row_copy(ys_ref, pos_ref[base + r], y_ref, r, sem).start()
        return carry

    def wait(r, carry):
        _row_copy(ys_ref, pos_ref[base + r], y_ref, r, sem).wait()
        return carry

    lax.fori_loop(0, tm, start, 0, unroll=DMA_UNROLL)
    lax.fori_loop(0, tm, wait, 0, unroll=DMA_UNROLL)
    for s in range(tm // ROW_BLOCK):
        rows = slice(s * ROW_BLOCK, (s + 1) * ROW_BLOCK)
        row = _mod_row(i * (tm // ROW_BLOCK) + s, nt, nctx, nb)
        o_ref[rows, :] = x_ref[rows, :] + g_ref[pl.ds(row, 1), :] * y_ref[rows, :]


def _combine(pos, x2, ys, mod_t, l, geo):
    m, d = x2.shape
    tm = _pick(m, (512, 256))
    kern = functools.partial(_combine_kernel, tm=tm, nt=geo["nt"], nctx=geo["nctx"], nb=geo["nb"])
    return pl.pallas_call(
        kern,
        grid_spec=pltpu.PrefetchScalarGridSpec(
            num_scalar_prefetch=1,
            grid=(m // tm,),
            in_specs=[
                pl.BlockSpec((tm, d), lambda i, pos: (i, 0)),
                pl.BlockSpec(memory_space=pl.ANY),
                pl.BlockSpec((None, None, 16, d), lambda i, pos: (l, 5, 0, 0)),
            ],
            out_specs=pl.BlockSpec((tm, d), lambda i, pos: (i, 0)),
            scratch_shapes=[pltpu.VMEM((tm, d), F32), pltpu.SemaphoreType.DMA(())],
        ),
        out_shape=jax.ShapeDtypeStruct((m, d), F32),
        input_output_aliases={1: 0},
        compiler_params=_params("arbitrary"),
        name="combine",
    )(pos, x2, ys, mod_t)


def _moe_ffn(x2, norm_g, mod_t, wr_hi, wr_lo, rbias, w_gate, w_up, w_down, l, geo):
    m, d = x2.shape
    ne = w_gate.shape[1]
    per = ne // N_GROUPS
    pairs = _pairs(per)
    ncls = N_GROUPS * len(pairs)
    tm = ROW_BLOCK
    nt = m // tm + ncls
    hs, meta, cnt = _ffnprep(x2, norm_g, mod_t, wr_hi, wr_lo, rbias, l, geo, ncls)

    count = cnt[:ncls, 0].astype(jnp.int32)
    tiles = (count + tm - 1) // tm
    tile_end = jnp.cumsum(tiles)
    row0 = (tile_end - tiles) * tm
    pos = row0[meta[0]] + meta[1]
    used = tile_end[-1]
    tidx = jnp.minimum(jnp.arange(nt, dtype=jnp.int32), used - 1)
    tile_cls = jnp.sum((tile_end[None, :] <= tidx[:, None]).astype(jnp.int32), axis=1)
    e_lo = jnp.array([g * per + a for g in range(N_GROUPS) for a, _ in pairs], jnp.int32)
    e_hi = jnp.array([g * per + c for g in range(N_GROUPS) for _, c in pairs], jnp.int32)
    eid = jnp.concatenate([e_lo[tile_cls], e_hi[tile_cls]])

    xs = _dispatch(pos, hs, jnp.zeros((nt * tm, d + LANES), F32))
    used = used.reshape(1)
    ys = _experts(eid, used, xs, w_gate, w_up, w_down, (), l, tm, 0)
    ys = _experts(eid, used, xs, w_gate, w_up, w_down, (ys,), l, tm, 1)
    return _combine(pos, x2, ys, mod_t, l, geo)


def _final_kernel(x_ref, g_ref, o_ref):
    o_ref[...] = _rms(x_ref[...]) * g_ref[...]


def _final_norm(x3, g, s_rows):
    b, lc, d = x3.shape
    tm = ROW_BLOCK
    return pl.pallas_call(
        _final_kernel,
        grid=(b, s_rows // tm),
        in_specs=[
            pl.BlockSpec((None, tm, d), lambda bi, i: (bi, i, 0)),
            pl.BlockSpec((1, d), lambda bi, i: (0, 0)),
        ],
        out_specs=pl.BlockSpec((None, tm, d), lambda bi, i: (bi, i, 0)),
        out_shape=jax.ShapeDtypeStruct((b, s_rows, d), F32),
        compiler_params=_params("arbitrary", "arbitrary"),
        name="final_norm",
    )(x3, g.reshape(1, d))


def _rope_tables(c_rows, s_rows):
    rows = s_rows // GRID_W
    row = jnp.broadcast_to(jnp.arange(rows)[:, None], (rows, GRID_W)).reshape(-1).astype(F32)
    col = jnp.broadcast_to(jnp.arange(GRID_W)[None, :], (rows, GRID_W)).reshape(-1).astype(F32)
    half = QK_ROPE // 2
    inv = ROPE_THETA ** (-jnp.arange(0, half, 2, dtype=F32) / half)
    ang_r = row[:, None] * inv
    ang_c = col[:, None] * inv
    ang = jnp.concatenate([ang_r, ang_r, ang_c, ang_c], axis=-1)
    ang = jnp.concatenate([ang, jnp.zeros((c_rows, QK_ROPE), F32)], axis=0)
    ang = jnp.concatenate([ang, ang], axis=-1)
    lane = jnp.arange(LANES)
    sign = jnp.where((lane % 32) < 16, -1.0, 1.0).astype(F32)
    return jnp.cos(ang), jnp.sin(ang) * sign


def kernel(x, c, ctx, c_ctx, w_mod, b_mod, norm_mix_g, norm_ffn_g, w_in, kv_norm_g, w_ukv, conv_w, conv_b,
           rg_wa, rg_ba, rg_wi, rg_bi, rg_lambda, w_o_mla, w_o_rnn, w_out, w_router, router_bias,
           w_gate, w_up, w_down, final_norm_g):
    b, s_rows, d = x.shape
    c_rows = ctx.shape[1]
    depth = w_mod.shape[0]
    kv_rank = kv_norm_g.shape[-1]
    lc = c_rows + s_rows
    m = b * lc
    assert c_rows % ROW_BLOCK == 0 and s_rows % c_rows == 0 and b < 16
    geo = {"nt": lc // ROW_BLOCK, "nctx": c_rows // ROW_BLOCK, "nb": b}

    nq = N_HEADS * QK_HEAD
    w_in_b = w_in.astype(BF16)
    wq = w_in_b[:, :, :nq].reshape(depth, d, N_HEADS, QK_HEAD)
    off_qr = N_HEADS * QK_NOPE
    off_ckv = off_qr + N_HEADS * QK_ROPE
    off_kr = off_ckv + kv_rank
    off_rx = -(-(off_kr + LANES) // d) * d
    off_ry, off_gm, off_gr = off_rx + d, off_rx + 2 * d, off_rx + 3 * d
    tail = nq + kv_rank + QK_ROPE
    w_in_p = jnp.concatenate([
        wq[..., :QK_NOPE].reshape(depth, d, -1),
        wq[..., QK_NOPE:].reshape(depth, d, -1),
        w_in_b[:, :, nq:tail],
        jnp.zeros((depth, d, off_rx - off_kr - QK_ROPE), BF16),
        w_in_b[:, :, tail:],
    ], axis=-1)
    wkv = w_ukv.reshape(depth, kv_rank, N_HEADS, QK_NOPE + V_HEAD)
    w_ukv_p = jnp.concatenate([wkv[..., :QK_NOPE].reshape(depth, kv_rank, -1),
                               wkv[..., QK_NOPE:].reshape(depth, kv_rank, -1)], axis=-1).astype(BF16)
    rg_w = (0.5 * jnp.concatenate([rg_wa, rg_wi], axis=-1)).astype(BF16)
    rg_p = jnp.stack([rg_ba, rg_bi, rg_lambda], axis=2)
    w_o_mla_b, w_o_rnn_b, w_out_b = w_o_mla.astype(BF16), w_o_rnn.astype(BF16), w_out.astype(BF16)
    w_gate_b, w_up_b, w_down_b = w_gate.astype(BF16), w_up.astype(BF16), w_down.astype(BF16)
    wr_t = w_router.T
    wr_hi = wr_t.astype(BF16)
    wr_lo = (wr_t - wr_hi.astype(F32)).astype(BF16)
    rbias = router_bias.reshape(-1, 1).astype(F32)
    norm_mix = norm_mix_g.reshape(depth, 1, d)
    norm_ffn = norm_ffn_g.reshape(depth, 1, d)
    kv_g = kv_norm_g.reshape(depth, 1, kv_rank)
    conv_b3 = conv_b.reshape(depth, 1, d)
    cos_t, sin_t = _rope_tables(c_rows, s_rows)

    cc = jnp.concatenate([c, c_ctx[None, :], jnp.zeros((16 - b - 1, d), F32)], axis=0)
    mod_t = _mod_table(cc, w_mod, b_mod).reshape(depth, 16, N_MOD, d).transpose(0, 2, 1, 3)

    x2 = jnp.concatenate([x, ctx], axis=1).reshape(m, d)
    for l in range(depth):
        p = _inproj(x2, norm_mix, mod_t, w_in_p, l, geo)
        p3 = p.reshape(b, lc, -1)
        kv = _kvup(p, kv_g, w_ukv_p, l, off_ckv)
        o = _attention(p3, kv.reshape(b, lc, -1), cos_t, sin_t, s_rows, off_qr, off_kr)
        hd = _rglru(p3, conv_w, conv_b3, rg_w, rg_p, l, geo, off_rx)
        z = _merge(o.reshape(m, -1), hd.reshape(2, m, d), p, w_o_mla_b, w_o_rnn_b, l, off_ry, off_gm, off_gr)
        x2 = _outproj(z, w_out_b, x2, mod_t, l, geo)
        x2 = _moe_ffn(x2, norm_ffn, mod_t, wr_hi, wr_lo, rbias, w_gate_b, w_up_b, w_down_b, l, geo)
    return _final_norm(x2.reshape(b, lc, d), final_norm_g, s_rows)
```
